```python
import math
import jax, jax.numpy as jnp
from jax import lax
import numpy as np

D_MODEL = 1024
BATCH = 16
SEQ = 4096
DEPTH = 4
DEC_BATCH = 16
DEC_SEQ = 32
PAST_LEN = 1024

CHUNK = 64
Q_BLOCK = 128
EPS = 1e-6
MLA_HEADS = 8
MLA_Q_RANK = 384
MLA_KV_RANK = 256
MLA_NOPE = 128
MLA_ROPE = 64
MLA_V = 128
ROPE_BASE = 10000.0
HG_HEADS = 4
HG_DK = 128
HG_DV = 128
FFN_DIM = 2816
CONV_W = 3

MLA_QK = MLA_NOPE + MLA_ROPE
MLA_VW = MLA_HEADS * MLA_V
HG_KW = HG_HEADS * HG_DK
HG_VW = HG_HEADS * HG_DV
IN_SPLITS = (MLA_Q_RANK, MLA_KV_RANK, MLA_ROPE, HG_KW, HG_KW, HG_VW, HG_VW, D_MODEL, D_MODEL)
IN_COLS = sum(IN_SPLITS)
IN_OFFSETS = tuple(int(o) for o in np.cumsum(IN_SPLITS)[:-1])

kernel_name = 'hybrid_mla_hgrn2_convffn_stream'


def rmsnorm(x, g):
    xf = x.astype(jnp.float32)
    y = xf * lax.rsqrt(jnp.mean(xf * xf, axis=-1, keepdims=True) + EPS)
    return (y * g.astype(jnp.float32)).astype(x.dtype)


def rope(x, pos):
    half = MLA_ROPE // 2
    inv = ROPE_BASE ** (-jnp.arange(half, dtype=jnp.float32) / half)
    ang = pos.astype(jnp.float32)[:, None] * inv[None, :]
    shape = (ang.shape[0],) + (1,) * (x.ndim - 3) + (half,)
    cos = jnp.cos(ang).reshape(shape)
    sin = jnp.sin(ang).reshape(shape)
    xf = x.astype(jnp.float32)
    x1, x2 = xf[..., :half], xf[..., half:]
    return jnp.concatenate([x1 * cos - x2 * sin, x2 * cos + x1 * sin], axis=-1).astype(x.dtype)


def mla_expand(latent, w_ukv):
    B, Lk, _ = latent.shape
    kv = jnp.einsum('blr,rc->blc', latent, w_ukv).reshape(B, Lk, MLA_HEADS, MLA_NOPE + MLA_V)
    return kv[..., :MLA_NOPE], kv[..., MLA_NOPE:]


def mla_attend(q_nope, q_pe, k_nope, k_pe, v, mask):
    scale = 1.0 / math.sqrt(MLA_QK)
    s = (jnp.einsum('bqhd,bkhd->bhqk', q_nope, k_nope)
         + jnp.einsum('bqhr,bkr->bhqk', q_pe, k_pe)).astype(jnp.float32) * scale
    if mask is not None:
        s = jnp.where(mask[None, None], s, -jnp.inf)
    p = jax.nn.softmax(s, axis=-1).astype(v.dtype)
    return jnp.einsum('bhqk,bkhd->bqhd', p, v)


def mla_prompt(q_nope, q_pe, k_nope, k_pe, v):
    B, L = q_nope.shape[0], q_nope.shape[1]
    nb = L // Q_BLOCK
    qn_b = q_nope.reshape(B, nb, Q_BLOCK, MLA_HEADS, MLA_NOPE).transpose(1, 0, 2, 3, 4)
    qp_b = q_pe.reshape(B, nb, Q_BLOCK, MLA_HEADS, MLA_ROPE).transpose(1, 0, 2, 3, 4)
    kchunk = jnp.arange(L) // CHUNK

    def blk(args):
        qn, qp, q0 = args
        qchunk = (q0 + jnp.arange(Q_BLOCK)) // CHUNK
        mask = kchunk[None, :] <= qchunk[:, None]
        return mla_attend(qn, qp, k_nope, k_pe, v, mask)

    out = lax.map(blk, (qn_b, qp_b, jnp.arange(nb) * Q_BLOCK))
    return out.transpose(1, 0, 2, 3, 4).reshape(B, L, MLA_VW)


def hgrn_scan(q, k, v, logf, S0):
    B, L, H, _ = q.shape
    c = min(CHUNK, L)
    n = L // c

    def to_chunks(a):
        return a.reshape(B, n, c, H, a.shape[-1]).transpose(1, 0, 3, 2, 4)

    causal = jnp.tril(jnp.ones((c, c), dtype=bool))

    def step(S, inp):
        qc, kc, vc, lfc = inp
        b = jnp.cumsum(lfc, axis=2)
        o_inter = jnp.einsum('bhtk,bhkv->bhtv', qc * jnp.exp(b), S)
        diff = b[:, :, :, None, :] - b[:, :, None, :, :]
        decay = jnp.exp(jnp.where(causal[:, :, None], diff, -jnp.inf))
        A = jnp.einsum('bhtk,bhtsk,bhsk->bhts', qc, decay, kc)
        o = o_inter + jnp.einsum('bhts,bhsv->bhtv', A, vc)
        b_end = b[:, :, -1:, :]
        S_new = (jnp.exp(b_end[:, :, 0, :])[..., None] * S
                 + jnp.einsum('bhsk,bhsv->bhkv', kc * jnp.exp(b_end - b), vc))
        return S_new, o

    S_fin, o = lax.scan(step, S0, (to_chunks(q), to_chunks(k), to_chunks(v), to_chunks(logf)))
    o = o.transpose(1, 0, 3, 2, 4).reshape(B, L, H, HG_DV)
    return o, S_fin


def conv_ffn(h, conv_state, w_up, conv_w, conv_b, w_down):
    L = h.shape[1]
    up = jnp.einsum('bld,df->blf', h, w_up)
    a, val = up[..., :FFN_DIM], up[..., FFN_DIM:]
    ext = jnp.concatenate([conv_state.astype(a.dtype), a], axis=1)
    conv = conv_b + sum(conv_w[j] * ext[:, j:j + L] for j in range(CONV_W))
    out = jnp.einsum('blf,fd->bld', jax.nn.gelu(conv) * val, w_down)
    return out, ext[:, -(CONV_W - 1):]


def trunk_layer(x, pos, past_latent, past_kpe, S0, conv_state, lb,
                norm_mix_l, w_in_l, q_norm_l, w_uq_l, kv_norm_l, w_ukv_l, hgrn_norm_l,
                w_proj_a_l, w_proj_b_l, w_out_l, norm_ffn_l, w_up_l, conv_w_l, conv_b_l, w_down_l):
    B, L, _ = x.shape
    h = rmsnorm(x, norm_mix_l)
    cq, ckv, kr, hq, hf, hi, hg, ga, gb = jnp.split(
        jnp.einsum('bld,dc->blc', h, w_in_l), IN_OFFSETS, axis=-1)

    q = jnp.einsum('blr,rc->blc', rmsnorm(cq, q_norm_l), w_uq_l).reshape(B, L, MLA_HEADS, MLA_QK)
    q_nope, q_pe = q[..., :MLA_NOPE], rope(q[..., MLA_NOPE:], pos)
    lat_new = rmsnorm(ckv, kv_norm_l)
    kpe_new = rope(kr, pos)
    if past_latent is None:
        k_nope, v = mla_expand(lat_new, w_ukv_l)
        o_a = mla_prompt(q_nope, q_pe, k_nope, kpe_new, v)
    else:
        lat_all = jnp.concatenate([past_latent.astype(lat_new.dtype), lat_new], axis=1)
        kpe_all = jnp.concatenate([past_kpe.astype(kpe_new.dtype), kpe_new], axis=1)
        k_nope, v = mla_expand(lat_all, w_ukv_l)
        o_a = mla_attend(q_nope, q_pe, k_nope, kpe_all, v, None).reshape(B, L, MLA_VW)

    f = lb + (1.0 - lb) * jax.nn.sigmoid(hf.astype(jnp.float32))
    q_h = (jax.nn.silu(hq.astype(jnp.float32)) * HG_DK ** -0.5).reshape(B, L, HG_HEADS, HG_DK)
    k_h = (1.0 - f).reshape(B, L, HG_HEADS, HG_DK)
    logf = jnp.log(f).reshape(B, L, HG_HEADS, HG_DK)
    v_h = hi.astype(jnp.float32).reshape(B, L, HG_HEADS, HG_DV)
    o_h, S_new = hgrn_scan(q_h, k_h, v_h, logf, S0.astype(jnp.float32))
    o_b = rmsnorm(o_h, hgrn_norm_l) * jax.nn.silu(hg.astype(jnp.float32)).reshape(B, L, HG_HEADS, HG_DV)
    o_b = o_b.reshape(B, L, HG_VW).astype(x.dtype)

    mix = (jax.nn.sigmoid(ga) * jnp.einsum('blc,cd->bld', o_a, w_proj_a_l)
           + jax.nn.sigmoid(gb) * jnp.einsum('blc,cd->bld', o_b, w_proj_b_l))
    x = x + jnp.einsum('bld,de->ble', mix, w_out_l)

    ffn_out, conv_new = conv_ffn(rmsnorm(x, norm_ffn_l), conv_state, w_up_l, conv_w_l, conv_b_l, w_down_l)
    x = x + ffn_out
    return x, lat_new, kpe_new, S_new.astype(x.dtype), conv_new


def setup_inputs(seed: int = 0) -> dict:
    key = jax.random.key(seed)
    ks = jax.random.split(key, 24)
    nrm = jax.random.normal
    f32 = jnp.float32

    def w(k, shape, fan_in):
        return nrm(k, shape, f32) * fan_in ** -0.5

    def gain(k, shape):
        return 1.0 + 0.02 * nrm(k, shape, f32)

    return {
        'x_prompt': nrm(ks[0], (BATCH, SEQ, D_MODEL), f32),
        'x_sample': nrm(ks[1], (DEC_BATCH, DEC_SEQ, D_MODEL), f32),
        'cache_mla_latent': nrm(ks[2], (DEPTH, DEC_BATCH, PAST_LEN, MLA_KV_RANK), f32),
        'cache_mla_krope': nrm(ks[3], (DEPTH, DEC_BATCH, PAST_LEN, MLA_ROPE), f32),
        'state_hgrn': 0.3 * nrm(ks[4], (DEPTH, DEC_BATCH, HG_HEADS, HG_DK, HG_DV), f32),
        'state_ffn_conv': nrm(ks[5], (DEPTH, DEC_BATCH, CONV_W - 1, FFN_DIM), f32),
        'norm_mix': gain(ks[6], (DEPTH, D_MODEL)),
        'w_in': w(ks[7], (DEPTH, D_MODEL, IN_COLS), D_MODEL),
        'q_norm': gain(ks[8], (DEPTH, MLA_Q_RANK)),
        'w_uq': w(ks[9], (DEPTH, MLA_Q_RANK, MLA_HEADS * MLA_QK), MLA_Q_RANK),
        'kv_norm': gain(ks[10], (DEPTH, MLA_KV_RANK)),
        'w_ukv': w(ks[11], (DEPTH, MLA_KV_RANK, MLA_HEADS * (MLA_NOPE + MLA_V)), MLA_KV_RANK),
        'lb_logits': 0.5 * nrm(ks[12], (DEPTH, HG_KW), f32),
        'hgrn_norm': gain(ks[13], (DEPTH, HG_DV)),
        'w_proj_a': w(ks[14], (DEPTH, MLA_VW, D_MODEL), MLA_VW),
        'w_proj_b': w(ks[15], (DEPTH, HG_VW, D_MODEL), HG_VW),
        'w_out': w(ks[16], (DEPTH, D_MODEL, D_MODEL), D_MODEL),
        'norm_ffn': gain(ks[17], (DEPTH, D_MODEL)),
        'w_up': w(ks[18], (DEPTH, D_MODEL, 2 * FFN_DIM), D_MODEL),
        'conv_w': w(ks[19], (DEPTH, CONV_W, FFN_DIM), CONV_W),
        'conv_b': 0.02 * nrm(ks[20], (DEPTH, FFN_DIM), f32),
        'w_down': w(ks[21], (DEPTH, FFN_DIM, D_MODEL), FFN_DIM),
        'norm_final': gain(ks[22], (D_MODEL,)),
    }


def reference(x_prompt, x_sample, cache_mla_latent, cache_mla_krope, state_hgrn, state_ffn_conv,
              norm_mix, w_in, q_norm, w_uq, kv_norm, w_ukv, lb_logits, hgrn_norm,
              w_proj_a, w_proj_b, w_out, norm_ffn, w_up, conv_w, conv_b, w_down, norm_final):
    lb_soft = jax.nn.softmax(lb_logits.astype(jnp.float32), axis=0)
    lower_bounds = jnp.cumsum(lb_soft, axis=0) - lb_soft[0:1]

    B, Lp = x_prompt.shape[0], x_prompt.shape[1]
    pos_p = jnp.arange(Lp)
    pos_s = PAST_LEN + jnp.arange(x_sample.shape[1])
    S0_p = jnp.zeros((B, HG_HEADS, HG_DK, HG_DV), x_prompt.dtype)
    conv0_p = jnp.zeros((B, CONV_W - 1, FFN_DIM), x_prompt.dtype)

    xp, xs = x_prompt, x_sample
    lat_p, kpe_p, hs_p, cv_p = [], [], [], []
    lat_s, kpe_s, hs_s, cv_s = [], [], [], []
    for l in range(DEPTH):
        weights = (norm_mix[l], w_in[l], q_norm[l], w_uq[l], kv_norm[l], w_ukv[l], hgrn_norm[l],
                   w_proj_a[l], w_proj_b[l], w_out[l], norm_ffn[l], w_up[l], conv_w[l], conv_b[l], w_down[l])
        xp, a1, a2, a3, a4 = trunk_layer(xp, pos_p, None, None, S0_p, conv0_p, lower_bounds[l], *weights)
        xs, b1, b2, b3, b4 = trunk_layer(xs, pos_s, cache_mla_latent[l], cache_mla_krope[l],
                                         state_hgrn[l], state_ffn_conv[l], lower_bounds[l], *weights)
        lat_p.append(a1); kpe_p.append(a2); hs_p.append(a3); cv_p.append(a4)
        lat_s.append(b1); kpe_s.append(b2); hs_s.append(b3); cv_s.append(b4)

    y_prompt = rmsnorm(xp, norm_final)
    y_sample = rmsnorm(xs, norm_final)
    return (y_prompt, y_sample,
            jnp.stack(lat_p), jnp.stack(kpe_p), jnp.stack(hs_p), jnp.stack(cv_p),
            jnp.stack(lat_s), jnp.stack(kpe_s), jnp.stack(hs_s), jnp.stack(cv_s))
```

```python
import functools
import math

import numpy as np
import jax
import jax.numpy as jnp
from jax import lax
from jax.experimental import pallas as pl
from jax.experimental.pallas import tpu as pltpu

D_MODEL = 1024
CHUNK = 64
EPS = 1e-6
MLA_HEADS = 8
MLA_Q_RANK = 384
MLA_KV_RANK = 256
MLA_NOPE = 128
MLA_ROPE = 64
MLA_V = 128
ROPE_BASE = 10000.0
HG_HEADS = 4
HG_DK = 128
HG_DV = 128
FFN_DIM = 2816
CONV_W = 3

MLA_QK = MLA_NOPE + MLA_ROPE
HG_W = HG_HEADS * HG_DK
QK_PAD = 2 * MLA_NOPE
LANES = 128
SUBLANES = 8
VMEM_LIMIT = 56 * 1024 * 1024

OFF_CQ = 0
OFF_CKV = OFF_CQ + MLA_Q_RANK
OFF_KR = OFF_CKV + MLA_KV_RANK
OFF_HG = OFF_KR + 2 * MLA_ROPE
OFF_GATE = OFF_HG + 4 * HG_W
IN_COLS_R = OFF_GATE + 2 * D_MODEL

BF16 = jnp.bfloat16
F32 = jnp.float32
NT_DIMS = (((1,), (1,)), ((), ()))
TN_DIMS = (((0,), (0,)), ((), ()))


def _dot(a, b):
    return jnp.dot(a, b, preferred_element_type=F32)


def _dot_nt(a, b):
    return lax.dot_general(a, b, NT_DIMS, preferred_element_type=F32)


def _rms(x, g):
    return x * lax.rsqrt(jnp.mean(x * x, axis=-1, keepdims=True) + EPS) * g


def _sigmoid(x):
    return 1.0 / (1.0 + jnp.exp(-x))


def _params(*sem):
    return pltpu.CompilerParams(dimension_semantics=sem, vmem_limit_bytes=VMEM_LIMIT)


def _inproj_kernel(x_ref, cos_ref, sin_ref, nmix_ref, win_ref, qn_ref, wuq_ref, kvn_ref, wuk_ref, wuvt_ref,
                   q_ref, k_ref, vt_ref, lat_ref, kpe_ref, h4_ref, g_ref, *, qscale):
    x = x_ref[0]
    hb = _rms(x, nmix_ref[...]).astype(BF16)
    cosf = cos_ref[...]
    sinf = sin_ref[...]
    lane = lax.broadcasted_iota(jnp.int32, cosf.shape, 1)
    first_half = (lane % MLA_ROPE) < (MLA_ROPE // 2)
    low_head = lane < MLA_ROPE

    def rot(v):
        sw = jnp.where(first_half, pltpu.roll(v, LANES - MLA_ROPE // 2, 1), pltpu.roll(v, MLA_ROPE // 2, 1))
        return v * cosf + sw * sinf

    cq = _dot(hb, win_ref[:, OFF_CQ:OFF_CKV])
    cqn = _rms(cq, qn_ref[...]).astype(BF16)
    q = _dot(cqn, wuq_ref[...]) * qscale

    ckv = _dot(hb, win_ref[:, OFF_CKV:OFF_KR])
    lat = _rms(ckv, kvn_ref[...])
    lat_ref[0] = lat
    latb = lat.astype(BF16)
    kn = _dot(latb, wuk_ref[...])
    vt_ref[0] = _dot_nt(wuvt_ref[...], latb).astype(BF16)

    kpe = rot(_dot(hb, win_ref[:, OFF_KR:OFF_HG]))
    kpe_ref[0] = kpe[:, :MLA_ROPE]
    kpeb = kpe.astype(BF16)

    nope_w = MLA_HEADS * MLA_NOPE
    for g in range(MLA_HEADS // 2):
        r = rot(q[:, nope_w + g * LANES: nope_w + (g + 1) * LANES])
        for hh in range(2):
            h = 2 * g + hh
            rh = jnp.where(low_head if hh == 0 else jnp.logical_not(low_head), r, 0.0)
            qh = jnp.concatenate([q[:, h * MLA_NOPE:(h + 1) * MLA_NOPE], rh], axis=1)
            q_ref[0, h] = qh.astype(BF16)
            k_ref[0, h] = jnp.concatenate([kn[:, h * MLA_NOPE:(h + 1) * MLA_NOPE].astype(BF16), kpeb], axis=1)

    h4_ref[0] = _dot(hb, win_ref[:, OFF_HG:OFF_GATE])
    g_ref[0] = _sigmoid(_dot(hb, win_ref[:, OFF_GATE:IN_COLS_R]))


def _inproj(x, cosf, sinf, nmix, win, qn, wuq, kvn, wuk, wuvt, tm):
    B, L, _ = x.shape
    nt = L // tm
    H = MLA_HEADS
    qscale = (1.0 / math.sqrt(MLA_QK)) * math.log2(math.e)
    full = lambda shape: pl.BlockSpec(shape, lambda b, t: (0,) * len(shape))
    out_shape = (
        jax.ShapeDtypeStruct((B, H, L, QK_PAD), BF16),
        jax.ShapeDtypeStruct((B, H, L, QK_PAD), BF16),
        jax.ShapeDtypeStruct((B, H * MLA_V, L), BF16),
        jax.ShapeDtypeStruct((B, L, MLA_KV_RANK), F32),
        jax.ShapeDtypeStruct((B, L, MLA_ROPE), F32),
        jax.ShapeDtypeStruct((B, L, 4 * HG_W), F32),
        jax.ShapeDtypeStruct((B, L, 2 * D_MODEL), F32),
    )
    out_specs = (
        pl.BlockSpec((1, H, tm, QK_PAD), lambda b, t: (b, 0, t, 0)),
        pl.BlockSpec((1, H, tm, QK_PAD), lambda b, t: (b, 0, t, 0)),
        pl.BlockSpec((1, H * MLA_V, tm), lambda b, t: (b, 0, t)),
        pl.BlockSpec((1, tm, MLA_KV_RANK), lambda b, t: (b, t, 0)),
        pl.BlockSpec((1, tm, MLA_ROPE), lambda b, t: (b, t, 0)),
        pl.BlockSpec((1, tm, 4 * HG_W), lambda b, t: (b, t, 0)),
        pl.BlockSpec((1, tm, 2 * D_MODEL), lambda b, t: (b, t, 0)),
    )
    in_specs = [
        pl.BlockSpec((1, tm, D_MODEL), lambda b, t: (b, t, 0)),
        pl.BlockSpec((tm, LANES), lambda b, t: (t, 0)),
        pl.BlockSpec((tm, LANES), lambda b, t: (t, 0)),
        full(nmix.shape), full(win.shape), full(qn.shape), full(wuq.shape), full(kvn.shape),
        full(wuk.shape), full(wuvt.shape),
    ]
    return pl.pallas_call(
        functools.partial(_inproj_kernel, qscale=qscale),
        out_shape=out_shape, grid=(B, nt), in_specs=in_specs, out_specs=out_specs,
        compiler_params=_params("parallel", "arbitrary"), name="inproj",
    )(x, cosf, sinf, nmix, win, qn, wuq, kvn, wuk, wuvt)


def _attn_kernel(q_ref, k_ref, vt_ref, o_ref, *, tq):
    i = pl.program_id(2)
    q = q_ref[0, 0]

    def step(j, carry, diagonal):
        m, l, acc = carry
        off = pl.multiple_of(j * tq, tq)
        kb = k_ref[0, 0, pl.ds(off, tq), :]
        s = _dot_nt(kb, q)
        if diagonal:
            kc = lax.broadcasted_iota(jnp.int32, s.shape, 0) // CHUNK
            qc = lax.broadcasted_iota(jnp.int32, s.shape, 1) // CHUNK
            s = jnp.where(kc <= qc, s, -jnp.inf)
        m_new = jnp.maximum(m, jnp.max(s, axis=0, keepdims=True))
        alpha = jnp.exp2(m - m_new)
        p = jnp.exp2(s - m_new)
        l = alpha * l + jnp.sum(p, axis=0, keepdims=True)
        vb = vt_ref[0, :, pl.ds(off, tq)]
        acc = alpha * acc + _dot(vb, p.astype(BF16))
        return m_new, l, acc

    init = (jnp.full((1, tq), -jnp.inf, F32), jnp.zeros((1, tq), F32), jnp.zeros((MLA_V, tq), F32))
    carry = lax.fori_loop(0, i, lambda j, c: step(j, c, False), init)
    _, l, acc = step(i, carry, True)
    o_ref[0] = (acc / l).T.astype(BF16)


def _attn_prompt(q, k, vt, tq):
    B, H, L, _ = q.shape
    return pl.pallas_call(
        functools.partial(_attn_kernel, tq=tq),
        out_shape=jax.ShapeDtypeStruct((B, L, H * MLA_V), BF16),
        grid=(B, H, L // tq),
        in_specs=[
            pl.BlockSpec((1, 1, tq, QK_PAD), lambda b, h, i: (b, h, i, 0)),
            pl.BlockSpec((1, 1, L, QK_PAD), lambda b, h, i: (b, h, 0, 0)),
            pl.BlockSpec((1, MLA_V, L), lambda b, h, i: (b, h, 0)),
        ],
        out_specs=pl.BlockSpec((1, tq, MLA_V), lambda b, h, i: (b, i, h)),
        compiler_params=_params("parallel", "parallel", "arbitrary"), name="attn_prompt",
    )(q, k, vt)


def _attn_sample_kernel(q_ref, kn_ref, latn_ref, latp_ref, kpep_ref, wuk_ref, wuv_ref, o_ref):
    latp = latp_ref[0].astype(BF16)
    latn = latn_ref[0].astype(BF16)
    knp = _dot(latp, wuk_ref[...]).astype(BF16)
    vp = _dot(latp, wuv_ref[...]).astype(BF16)
    vn = _dot(latn, wuv_ref[...]).astype(BF16)
    kp = kpep_ref[0].astype(BF16)
    kpd = jnp.concatenate([kp, kp], axis=1)
    for h in range(MLA_HEADS):
        sl = slice(h * MLA_NOPE, (h + 1) * MLA_NOPE)
        qh = q_ref[0, h]
        khp = jnp.concatenate([knp[:, sl], kpd], axis=1)
        s1 = _dot_nt(qh, khp)
        s2 = _dot_nt(qh, kn_ref[0, h])
        m = jnp.maximum(jnp.max(s1, axis=1, keepdims=True), jnp.max(s2, axis=1, keepdims=True))
        p1 = jnp.exp2(s1 - m)
        p2 = jnp.exp2(s2 - m)
        l = jnp.sum(p1, axis=1, keepdims=True) + jnp.sum(p2, axis=1, keepdims=True)
        o = _dot(p1.astype(BF16), vp[:, sl]) + _dot(p2.astype(BF16), vn[:, sl])
        o_ref[0, :, sl] = (o / l).astype(BF16)


def _attn_sample(q, kn, latn, latp, kpep, wuk, wuv):
    B, H, Ls, _ = q.shape
    P = latp.shape[1]
    full = lambda shape: pl.BlockSpec(shape, lambda b: (0,) * len(shape))
    return pl.pallas_call(
        _attn_sample_kernel,
        out_shape=jax.ShapeDtypeStruct((B, Ls, H * MLA_V), BF16),
        grid=(B,),
        in_specs=[
            pl.BlockSpec((1, H, Ls, QK_PAD), lambda b: (b, 0, 0, 0)),
            pl.BlockSpec((1, H, Ls, QK_PAD), lambda b: (b, 0, 0, 0)),
            pl.BlockSpec((1, Ls, MLA_KV_RANK), lambda b: (b, 0, 0)),
            pl.BlockSpec((1, P, MLA_KV_RANK), lambda b: (b, 0, 0)),
            pl.BlockSpec((1, P, MLA_ROPE), lambda b: (b, 0, 0)),
            full(wuk.shape), full(wuv.shape),
        ],
        out_specs=pl.BlockSpec((1, Ls, H * MLA_V), lambda b: (b, 0, 0)),
        compiler_params=_params("parallel"), name="attn_sample",
    )(q, kn, latn, latp, kpep, wuk, wuv)


DIAG_BLOCK = SUBLANES


def _hgrn_kernel(h4_ref, lbl_ref, gn_ref, s0_ref, o_ref, sout_ref, st_ref, *, layer, c, nchunk):
    t = pl.program_id(1)

    @pl.when(t == 0)
    def _():
        for h in range(HG_HEADS):
            st_ref[h] = s0_ref[0, h].T

    lg = lbl_ref[...]
    e = jnp.exp(lg - jnp.max(lg, axis=0, keepdims=True))
    sm = e / jnp.sum(e, axis=0, keepdims=True)
    lb_all = sm[0:1] * 0.0
    for i in range(1, layer + 1):
        lb_all = lb_all + sm[i:i + 1]
    gn = gn_ref[...]

    rt = lax.broadcasted_iota(jnp.int32, (c, c), 0)
    ct = lax.broadcasted_iota(jnp.int32, (c, c), 1)
    tril = jnp.where(ct <= rt, 1.0, 0.0).astype(BF16)
    row = lax.broadcasted_iota(jnp.int32, (c, HG_DK), 0)
    qk_scale = HG_DK ** -0.5

    def chunk_body(ci, carry):
        r0 = pl.multiple_of(ci * c, c)
        for h in range(HG_HEADS):
            def seg(k):
                return h4_ref[0, pl.ds(r0, c), pl.ds(k * HG_W + h * HG_DK, HG_DK)]
            hq, hf, hi, hg = seg(0), seg(1), seg(2), seg(3)
            lb = lb_all[:, h * HG_DK:(h + 1) * HG_DK]
            f = lb + (1.0 - lb) * _sigmoid(hf)
            logf = jnp.log(f)
            kk = 1.0 - f
            qq = hq * _sigmoid(hq) * qk_scale
            vb = hi.astype(BF16)

            p0 = logf.astype(BF16)
            r1 = logf - p0.astype(F32)
            p1 = r1.astype(BF16)
            p2 = (r1 - p1.astype(F32)).astype(BF16)
            b = _dot(tril, p0) + _dot(tril, p1) + _dot(tril, p2)
            bend = b[c - 1:c, :]

            st = st_ref[h]
            o = _dot_nt((qq * jnp.exp(b)).astype(BF16), st.astype(BF16))
            kd = (kk * jnp.exp(bend - b)).astype(BF16)
            st_ref[h] = st * jnp.exp(bend) + lax.dot_general(vb, kd, TN_DIMS, preferred_element_type=F32)

            a = jnp.zeros((c, c), F32)
            n = c // 2
            while n >= DIAG_BLOCK:
                qparts, kparts = [], []
                for j in range(c // (2 * n)):
                    lo, mid, hi_ = 2 * n * j, 2 * n * j + n, 2 * n * (j + 1)
                    ref = b[mid - 1:mid, :]
                    qparts += [jnp.zeros((n, HG_DK), F32), qq[mid:hi_] * jnp.exp(b[mid:hi_] - ref)]
                    kparts += [kk[lo:mid] * jnp.exp(ref - b[lo:mid]), jnp.zeros((n, HG_DK), F32)]
                an = _dot_nt(jnp.concatenate(qparts, axis=0).astype(BF16),
                             jnp.concatenate(kparts, axis=0).astype(BF16))
                a = a + jnp.where((rt // (2 * n)) == (ct // (2 * n)), an, 0.0)
                n //= 2
            o = o + _dot(a.astype(BF16), vb)

            for d in range(DIAG_BLOCK):
                if d == 0:
                    w = qq * kk
                    vd = hi
                else:
                    arg = jnp.where((row % DIAG_BLOCK) >= d, b - pltpu.roll(b, d, 0), -jnp.inf)
                    w = qq * pltpu.roll(kk, d, 0) * jnp.exp(arg)
                    vd = pltpu.roll(hi, d, 0)
                o = o + jnp.sum(w, axis=1, keepdims=True) * vd

            ob = _rms(o, gn) * (hg * _sigmoid(hg))
            o_ref[0, pl.ds(r0, c), pl.ds(h * HG_DV, HG_DV)] = ob.astype(BF16)
        return carry

    lax.fori_loop(0, nchunk, chunk_body, 0)

    @pl.when(t == pl.num_programs(1) - 1)
    def _():
        for h in range(HG_HEADS):
            sout_ref[0, h] = st_ref[h].T


def _hgrn(h4, lb_logits, gnorm, s0, layer, tt):
    B, L, _ = h4.shape
    c = min(CHUNK, L)
    tt = min(tt, L)
    full = lambda shape: pl.BlockSpec(shape, lambda b, t: (0,) * len(shape))
    return pl.pallas_call(
        functools.partial(_hgrn_kernel, layer=layer, c=c, nchunk=tt // c),
        out_shape=(jax.ShapeDtypeStruct((B, L, HG_HEADS * HG_DV), BF16),
                   jax.ShapeDtypeStruct((B, HG_HEADS, HG_DK, HG_DV), F32)),
        grid=(B, L // tt),
        in_specs=[
            pl.BlockSpec((1, tt, 4 * HG_W), lambda b, t: (b, t, 0)),
            full(lb_logits.shape), full(gnorm.shape),
            pl.BlockSpec((1, HG_HEADS, HG_DK, HG_DV), lambda b, t: (b, 0, 0, 0)),
        ],
        out_specs=(pl.BlockSpec((1, tt, HG_HEADS * HG_DV), lambda b, t: (b, t, 0)),
                   pl.BlockSpec((1, HG_HEADS, HG_DK, HG_DV), lambda b, t: (b, 0, 0, 0))),
        scratch_shapes=[pltpu.VMEM((HG_HEADS, HG_DV, HG_DK), F32)],
        compiler_params=_params("parallel", "arbitrary"), name="hgrn",
    )(h4, lb_logits, gnorm, s0)


def _merge_kernel(x_ref, oa_ref, ob_ref, g_ref, wpa_ref, wpb_ref, wout_ref, y_ref):
    ga = g_ref[0, :, :D_MODEL]
    gb = g_ref[0, :, D_MODEL:]
    mix = ga * _dot(oa_ref[0], wpa_ref[...]) + gb * _dot(ob_ref[0], wpb_ref[...])
    y_ref[0] = x_ref[0] + _dot(mix.astype(BF16), wout_ref[...])


def _merge(x, oa, ob, g, wpa, wpb, wout, tm):
    B, L, _ = x.shape
    full = lambda shape: pl.BlockSpec(shape, lambda b, t: (0,) * len(shape))
    row = lambda w: pl.BlockSpec((1, tm, w), lambda b, t: (b, t, 0))
    return pl.pallas_call(
        _merge_kernel,
        out_shape=jax.ShapeDtypeStruct(x.shape, F32),
        grid=(B, L // tm),
        in_specs=[row(D_MODEL), row(oa.shape[2]), row(ob.shape[2]), row(2 * D_MODEL),
                  full(wpa.shape), full(wpb.shape), full(wout.shape)],
        out_specs=row(D_MODEL),
        compiler_params=_params("parallel", "parallel"), name="merge",
    )(x, oa, ob, g, wpa, wpb, wout)


FFN_SPLIT = 2
FFN_COLS = FFN_DIM // FFN_SPLIT


def _gelu_tanh(x):
    return 0.5 * x * (1.0 + jnp.tanh(math.sqrt(2.0 / math.pi) * (x + 0.044715 * (x * x * x))))


def _ffn_kernel(x_ref, cs_ref, nf_ref, wup_ref, cw_ref, cb_ref, wdn_ref, nfin_ref, y_ref, cnew_ref, carry_ref,
                *, final, tm):
    t = pl.program_id(1)

    @pl.when(t == 0)
    def _():
        carry_ref[0:CONV_W - 1, :] = cs_ref[0]

    x = x_ref[0]
    hb = _rms(x, nf_ref[...]).astype(BF16)
    row = lax.broadcasted_iota(jnp.int32, (tm, FFN_COLS), 0)
    acc = x
    for c in range(FFN_SPLIT):
        cs = slice(c * FFN_COLS, (c + 1) * FFN_COLS)
        a = _dot(hb, wup_ref[:, c * FFN_COLS:(c + 1) * FFN_COLS])
        val = _dot(hb, wup_ref[:, FFN_DIM + c * FFN_COLS:FFN_DIM + (c + 1) * FFN_COLS])
        prev2 = carry_ref[0:1, cs]
        prev1 = carry_ref[1:2, cs]
        a1 = jnp.where(row == 0, prev1, pltpu.roll(a, 1, 0))
        a2 = jnp.where(row == 0, prev2, jnp.where(row == 1, prev1, pltpu.roll(a, 2, 0)))
        conv = cb_ref[:, cs] + cw_ref[0:1, cs] * a2 + cw_ref[1:2, cs] * a1 + cw_ref[2:3, cs] * a
        gated = (_gelu_tanh(conv) * val).astype(BF16)
        acc = acc + _dot(gated, wdn_ref[c * FFN_COLS:(c + 1) * FFN_COLS, :])
        carry_ref[0:CONV_W - 1, cs] = a[tm - (CONV_W - 1):tm, :]
    if final:
        acc = _rms(acc, nfin_ref[...])
    y_ref[0] = acc

    @pl.when(t == pl.num_programs(1) - 1)
    def _():
        cnew_ref[0] = carry_ref[0:CONV_W - 1, :]


def _ffn(x, conv_state, nf, wup, cw, cb, wdn, nfin, final, tm):
    B, L, _ = x.shape
    full = lambda shape: pl.BlockSpec(shape, lambda b, t: (0,) * len(shape))
    return pl.pallas_call(
        functools.partial(_ffn_kernel, final=final, tm=tm),
        out_shape=(jax.ShapeDtypeStruct(x.shape, F32),
                   jax.ShapeDtypeStruct((B, CONV_W - 1, FFN_DIM), F32)),
        grid=(B, L // tm),
        in_specs=[
            pl.BlockSpec((1, tm, D_MODEL), lambda b, t: (b, t, 0)),
            pl.BlockSpec((1, CONV_W - 1, FFN_DIM), lambda b, t: (b, 0, 0)),
            full(nf.shape), full(wup.shape), full(cw.shape), full(cb.shape), full(wdn.shape), full(nfin.shape),
        ],
        out_specs=(pl.BlockSpec((1, tm, D_MODEL), lambda b, t: (b, t, 0)),
                   pl.BlockSpec((1, CONV_W - 1, FFN_DIM), lambda b, t: (b, 0, 0))),
        scratch_shapes=[pltpu.VMEM((SUBLANES, FFN_DIM), F32)],
        compiler_params=_params("parallel", "arbitrary"), name="ffn",
    )(x, conv_state, nf, wup, cw, cb, wdn, nfin)


def _rope_tables(pos):
    half = MLA_ROPE // 2
    inv = ROPE_BASE ** (-jnp.arange(half, dtype=F32) / half)
    ang = pos.astype(F32)[:, None] * inv[None, :]
    cos, sin = jnp.cos(ang), jnp.sin(ang)
    return jnp.tile(cos, (1, 4)), jnp.concatenate([-sin, sin, -sin, sin], axis=1)


def _prep_layer(w_in_l, w_uq_l, w_ukv_l):
    kr = w_in_l[:, OFF_KR:OFF_KR + MLA_ROPE]
    win = jnp.concatenate([w_in_l[:, :OFF_KR], kr, kr, w_in_l[:, OFF_KR + MLA_ROPE:]], axis=1).astype(BF16)
    uq = w_uq_l.reshape(MLA_Q_RANK, MLA_HEADS, MLA_QK)
    wuq = jnp.concatenate([uq[:, :, :MLA_NOPE].reshape(MLA_Q_RANK, -1),
                           uq[:, :, MLA_NOPE:].reshape(MLA_Q_RANK, -1)], axis=1).astype(BF16)
    ukv = w_ukv_l.reshape(MLA_KV_RANK, MLA_HEADS, MLA_NOPE + MLA_V)
    wuk = ukv[:, :, :MLA_NOPE].reshape(MLA_KV_RANK, -1).astype(BF16)
    wuv = ukv[:, :, MLA_NOPE:].reshape(MLA_KV_RANK, -1).astype(BF16)
    return win, wuq, wuk, wuv


def _tile(L, pref):
    return pref if L % pref == 0 else L


def kernel(x_prompt, x_sample, cache_mla_latent, cache_mla_krope, state_hgrn, state_ffn_conv, norm_mix, w_in, q_norm, w_uq, kv_norm, w_ukv, lb_logits, hgrn_norm, w_proj_a, w_proj_b, w_out, norm_ffn, w_up, conv_w, conv_b, w_down, norm_final):
    depth = w_in.shape[0]
    B, Lp, _ = x_prompt.shape
    Bs, Ls, _ = x_sample.shape
    past = cache_mla_latent.shape[2]

    cos_p, sin_p = _rope_tables(jnp.arange(Lp))
    cos_s, sin_s = _rope_tables(past + jnp.arange(Ls))
    s0_p = jnp.zeros((B, HG_HEADS, HG_DK, HG_DV), F32)
    conv0_p = jnp.zeros((B, CONV_W - 1, FFN_DIM), F32)
    nfin = norm_final.reshape(1, D_MODEL)
    lbl = lb_logits.astype(F32)

    tm_p, tm_s = _tile(Lp, 256), _tile(Ls, 256)
    tq = _tile(Lp, 256)

    xp, xs = x_prompt, x_sample
    outs_p, outs_s = [], []
    for l in range(depth):
        win, wuq, wuk, wuv = _prep_layer(w_in[l], w_uq[l], w_ukv[l])
        wuvt = wuv.T
        nmix = norm_mix[l].reshape(1, -1)
        qn = q_norm[l].reshape(1, -1)
        kvn = kv_norm[l].reshape(1, -1)
        gn = hgrn_norm[l].reshape(1, -1)
        nf = norm_ffn[l].reshape(1, -1)
        wpa, wpb, wo = w_proj_a[l].astype(BF16), w_proj_b[l].astype(BF16), w_out[l].astype(BF16)
        wup, wdn = w_up[l].astype(BF16), w_down[l].astype(BF16)
        cw, cb = conv_w[l], conv_b[l].reshape(1, -1)
        final = l == depth - 1

        q, k, vt, lat, kpe, h4, g = _inproj(xp, cos_p, sin_p, nmix, win, qn, wuq, kvn, wuk, wuvt, tm_p)
        oa = _attn_prompt(q, k, vt, tq)
        ob, s_new = _hgrn(h4, lbl, gn, s0_p, l, 512)
        x1 = _merge(xp, oa, ob, g, wpa, wpb, wo, tm_p)
        xp, cnew = _ffn(x1, conv0_p, nf, wup, cw, cb, wdn, nfin, final, tm_p)
        outs_p.append((lat, kpe, s_new, cnew))

        q, k, vt, lat, kpe, h4, g = _inproj(xs, cos_s, sin_s, nmix, win, qn, wuq, kvn, wuk, wuvt, tm_s)
        oa = _attn_sample(q, k, lat, cache_mla_latent[l], cache_mla_krope[l], wuk, wuv)
        ob, s_new = _hgrn(h4, lbl, gn, state_hgrn[l], l, 512)
        x1 = _merge(xs, oa, ob, g, wpa, wpb, wo, tm_s)
        xs, cnew = _ffn(x1, state_ffn_conv[l], nf, wup, cw, cb, wdn, nfin, final, tm_s)
        outs_s.append((lat, kpe, s_new, cnew))

    stack = lambda outs, i: jnp.stack([o[i] for o in outs])
    return (xp, xs,
            stack(outs_p, 0), stack(outs_p, 1), stack(outs_p, 2), stack(outs_p, 3),
            stack(outs_s, 0), stack(outs_s, 1), stack(outs_s, 2), stack(outs_s, 3))
```

```python
import functools
import math

import numpy as np
import jax
import jax.numpy as jnp
from jax import lax
from jax.experimental import pallas as pl
from jax.experimental.pallas import tpu as pltpu

D_MODEL = 1024
CHUNK = 64
EPS = 1e-6
MLA_HEADS = 8
MLA_Q_RANK = 384
MLA_KV_RANK = 256
MLA_NOPE = 128
MLA_ROPE = 64
MLA_V = 128
ROPE_BASE = 10000.0
HG_HEADS = 4
HG_DK = 128
HG_DV = 128
FFN_DIM = 2816
CONV_W = 3

MLA_QK = MLA_NOPE + MLA_ROPE
HG_W = HG_HEADS * HG_DK
QK_PAD = 2 * MLA_NOPE
LANES = 128
SUBLANES = 8
VMEM_LIMIT = 56 * 1024 * 1024

OFF_CQ = 0
OFF_CKV = OFF_CQ + MLA_Q_RANK
OFF_KR = OFF_CKV + MLA_KV_RANK
OFF_HG = OFF_KR + 2 * MLA_ROPE
OFF_GATE = OFF_HG + 4 * HG_W
IN_COLS_R = OFF_GATE + 2 * D_MODEL

BF16 = jnp.bfloat16
F32 = jnp.float32
NT_DIMS = (((1,), (1,)), ((), ()))
TN_DIMS = (((0,), (0,)), ((), ()))


def _dot(a, b):
    return jnp.dot(a, b, preferred_element_type=F32)


def _dot_nt(a, b):
    return lax.dot_general(a, b, NT_DIMS, preferred_element_type=F32)


def _rms(x, g):
    return x * lax.rsqrt(jnp.mean(x * x, axis=-1, keepdims=True) + EPS) * g


def _sigmoid(x):
    return 1.0 / (1.0 + jnp.exp(-x))


def _params(*sem):
    return pltpu.CompilerParams(dimension_semantics=sem, vmem_limit_bytes=VMEM_LIMIT)


def _inproj_kernel(x_ref, cos_ref, sin_ref, nmix_ref, win_ref, qn_ref, wuq_ref, kvn_ref, wuk_ref, wuvt_ref,
                   q_ref, k_ref, vt_ref, lat_ref, kpe_ref, h4_ref, g_ref, *, qscale):
    x = x_ref[0]
    hb = _rms(x, nmix_ref[...]).astype(BF16)
    cosf = cos_ref[...]
    sinf = sin_ref[...]
    lane = lax.broadcasted_iota(jnp.int32, cosf.shape, 1)
    first_half = (lane % MLA_ROPE) < (MLA_ROPE // 2)
    low_head = lane < MLA_ROPE

    def rot(v):
        sw = jnp.where(first_half, pltpu.roll(v, LANES - MLA_ROPE // 2, 1), pltpu.roll(v, MLA_ROPE // 2, 1))
        return v * cosf + sw * sinf

    cq = _dot(hb, win_ref[:, OFF_CQ:OFF_CKV])
    cqn = _rms(cq, qn_ref[...]).astype(BF16)
    q = _dot(cqn, wuq_ref[...]) * qscale

    ckv = _dot(hb, win_ref[:, OFF_CKV:OFF_KR])
    lat = _rms(ckv, kvn_ref[...])
    lat_ref[0] = lat
    latb = lat.astype(BF16)
    kn = _dot(latb, wuk_ref[...])
    vt_ref[0] = _dot_nt(wuvt_ref[...], latb).astype(BF16)

    kpe = rot(_dot(hb, win_ref[:, OFF_KR:OFF_HG]))
    kpe_ref[0] = kpe[:, :MLA_ROPE]
    kpeb = kpe.astype(BF16)

    nope_w = MLA_HEADS * MLA_NOPE
    for g in range(MLA_HEADS // 2):
        r = rot(q[:, nope_w + g * LANES: nope_w + (g + 1) * LANES])
        for hh in range(2):
            h = 2 * g + hh
            rh = jnp.where(low_head if hh == 0 else jnp.logical_not(low_head), r, 0.0)
            qh = jnp.concatenate([q[:, h * MLA_NOPE:(h + 1) * MLA_NOPE], rh], axis=1)
            q_ref[0, h] = qh.astype(BF16)
            k_ref[0, h] = jnp.concatenate([kn[:, h * MLA_NOPE:(h + 1) * MLA_NOPE].astype(BF16), kpeb], axis=1)

    h4_ref[0] = _dot(hb, win_ref[:, OFF_HG:OFF_GATE])
    g_ref[0] = _sigmoid(_dot(hb, win_ref[:, OFF_GATE:IN_COLS_R]))


def _inproj(x, cosf, sinf, nmix, win, qn, wuq, kvn, wuk, wuvt, tm):
    B, L, _ = x.shape
    nt = L // tm
    H = MLA_HEADS
    qscale = (1.0 / math.sqrt(MLA_QK)) * math.log2(math.e)
    full = lambda shape: pl.BlockSpec(shape, lambda b, t: (0,) * len(shape))
    out_shape = (
        jax.ShapeDtypeStruct((B, H, L, QK_PAD), BF16),
        jax.ShapeDtypeStruct((B, H, L, QK_PAD), BF16),
        jax.ShapeDtypeStruct((B, H * MLA_V, L), BF16),
        jax.ShapeDtypeStruct((B, L, MLA_KV_RANK), F32),
        jax.ShapeDtypeStruct((B, L, MLA_ROPE), F32),
        jax.ShapeDtypeStruct((B, L, 4 * HG_W), F32),
        jax.ShapeDtypeStruct((B, L, 2 * D_MODEL), F32),
    )
    out_specs = (
        pl.BlockSpec((1, H, tm, QK_PAD), lambda b, t: (b, 0, t, 0)),
        pl.BlockSpec((1, H, tm, QK_PAD), lambda b, t: (b, 0, t, 0)),
        pl.BlockSpec((1, H * MLA_V, tm), lambda b, t: (b, 0, t)),
        pl.BlockSpec((1, tm, MLA_KV_RANK), lambda b, t: (b, t, 0)),
        pl.BlockSpec((1, tm, MLA_ROPE), lambda b, t: (b, t, 0)),
        pl.BlockSpec((1, tm, 4 * HG_W), lambda b, t: (b, t, 0)),
        pl.BlockSpec((1, tm, 2 * D_MODEL), lambda b, t: (b, t, 0)),
    )
    in_specs = [
        pl.BlockSpec((1, tm, D_MODEL), lambda b, t: (b, t, 0)),
        pl.BlockSpec((tm, LANES), lambda b, t: (t, 0)),
        pl.BlockSpec((tm, LANES), lambda b, t: (t, 0)),
        full(nmix.shape), full(win.shape), full(qn.shape), full(wuq.shape), full(kvn.shape),
        full(wuk.shape), full(wuvt.shape),
    ]
    return pl.pallas_call(
        functools.partial(_inproj_kernel, qscale=qscale),
        out_shape=out_shape, grid=(B, nt), in_specs=in_specs, out_specs=out_specs,
        compiler_params=_params("parallel", "arbitrary"), name="inproj",
    )(x, cosf, sinf, nmix, win, qn, wuq, kvn, wuk, wuvt)


ATTN_HEAD_GROUP = 4


def _attn_kernel(q_ref, k_ref, vt_ref, o_ref, acc_ref, *, tq):
    i = pl.program_id(2)
    G = ATTN_HEAD_GROUP

    def step(j, carry, diagonal):
        off = pl.multiple_of(j * tq, tq)
        out = []
        for g in range(G):
            m, l = carry[g]
            kb = k_ref[0, g, pl.ds(off, tq), :]
            s = _dot_nt(kb, q_ref[0, g])
            if diagonal:
                kc = lax.broadcasted_iota(jnp.int32, s.shape, 0) // CHUNK
                qc = lax.broadcasted_iota(jnp.int32, s.shape, 1) // CHUNK
                s = jnp.where(kc <= qc, s, -jnp.inf)
            m_new = jnp.maximum(m, jnp.max(s, axis=0, keepdims=True))
            alpha = jnp.exp2(m - m_new)
            p = jnp.exp2(s - m_new)
            l = alpha * l + jnp.sum(p, axis=0, keepdims=True)
            vb = vt_ref[0, g * MLA_V:(g + 1) * MLA_V, pl.ds(off, tq)]
            acc_ref[g] = alpha * acc_ref[g] + _dot(vb, p.astype(BF16))
            out.append((m_new, l))
        return tuple(out)

    acc_ref[...] = jnp.zeros_like(acc_ref)
    init = tuple((jnp.full((1, tq), -jnp.inf, F32), jnp.zeros((1, tq), F32)) for _ in range(G))
    carry = lax.fori_loop(0, i, lambda j, c: step(j, c, False), init)
    carry = step(i, carry, True)
    for g in range(G):
        o_ref[0, :, g * MLA_V:(g + 1) * MLA_V] = (acc_ref[g] / carry[g][1]).T.astype(BF16)


def _attn_prompt(q, k, vt, tq):
    B, H, L, _ = q.shape
    G = ATTN_HEAD_GROUP
    return pl.pallas_call(
        functools.partial(_attn_kernel, tq=tq),
        out_shape=jax.ShapeDtypeStruct((B, L, H * MLA_V), BF16),
        grid=(B, H // G, L // tq),
        in_specs=[
            pl.BlockSpec((1, G, tq, QK_PAD), lambda b, h, i: (b, h, i, 0)),
            pl.BlockSpec((1, G, L, QK_PAD), lambda b, h, i: (b, h, 0, 0)),
            pl.BlockSpec((1, G * MLA_V, L), lambda b, h, i: (b, h, 0)),
        ],
        out_specs=pl.BlockSpec((1, tq, G * MLA_V), lambda b, h, i: (b, i, h)),
        scratch_shapes=[pltpu.VMEM((G, MLA_V, tq), F32)],
        compiler_params=_params("parallel", "parallel", "arbitrary"), name="attn_prompt",
    )(q, k, vt)


def _attn_sample_kernel(q_ref, kn_ref, latn_ref, latp_ref, kpep_ref, wuk_ref, wuv_ref, o_ref):
    latp = latp_ref[0].astype(BF16)
    latn = latn_ref[0].astype(BF16)
    knp = _dot(latp, wuk_ref[...]).astype(BF16)
    vp = _dot(latp, wuv_ref[...]).astype(BF16)
    vn = _dot(latn, wuv_ref[...]).astype(BF16)
    kp = kpep_ref[0].astype(BF16)
    kpd = jnp.concatenate([kp, kp], axis=1)
    for h in range(MLA_HEADS):
        sl = slice(h * MLA_NOPE, (h + 1) * MLA_NOPE)
        qh = q_ref[0, h]
        khp = jnp.concatenate([knp[:, sl], kpd], axis=1)
        s1 = _dot_nt(qh, khp)
        s2 = _dot_nt(qh, kn_ref[0, h])
        m = jnp.maximum(jnp.max(s1, axis=1, keepdims=True), jnp.max(s2, axis=1, keepdims=True))
        p1 = jnp.exp2(s1 - m)
        p2 = jnp.exp2(s2 - m)
        l = jnp.sum(p1, axis=1, keepdims=True) + jnp.sum(p2, axis=1, keepdims=True)
        o = _dot(p1.astype(BF16), vp[:, sl]) + _dot(p2.astype(BF16), vn[:, sl])
        o_ref[0, :, sl] = (o / l).astype(BF16)


def _attn_sample(q, kn, latn, latp, kpep, wuk, wuv):
    B, H, Ls, _ = q.shape
    P = latp.shape[1]
    full = lambda shape: pl.BlockSpec(shape, lambda b: (0,) * len(shape))
    return pl.pallas_call(
        _attn_sample_kernel,
        out_shape=jax.ShapeDtypeStruct((B, Ls, H * MLA_V), BF16),
        grid=(B,),
        in_specs=[
            pl.BlockSpec((1, H, Ls, QK_PAD), lambda b: (b, 0, 0, 0)),
            pl.BlockSpec((1, H, Ls, QK_PAD), lambda b: (b, 0, 0, 0)),
            pl.BlockSpec((1, Ls, MLA_KV_RANK), lambda b: (b, 0, 0)),
            pl.BlockSpec((1, P, MLA_KV_RANK), lambda b: (b, 0, 0)),
            pl.BlockSpec((1, P, MLA_ROPE), lambda b: (b, 0, 0)),
            full(wuk.shape), full(wuv.shape),
        ],
        out_specs=pl.BlockSpec((1, Ls, H * MLA_V), lambda b: (b, 0, 0)),
        compiler_params=_params("parallel"), name="attn_sample",
    )(q, kn, latn, latp, kpep, wuk, wuv)


DIAG_BLOCK = SUBLANES


def _hgrn_kernel(h4_ref, lbl_ref, gn_ref, s0_ref, o_ref, sout_ref, st_ref, *, layer, c, nchunk):
    t = pl.program_id(1)

    @pl.when(t == 0)
    def _():
        for h in range(HG_HEADS):
            st_ref[h] = s0_ref[0, h].T

    lg = lbl_ref[...]
    e = jnp.exp(lg - jnp.max(lg, axis=0, keepdims=True))
    sm = e / jnp.sum(e, axis=0, keepdims=True)
    lb_all = sm[0:1] * 0.0
    for i in range(1, layer + 1):
        lb_all = lb_all + sm[i:i + 1]
    gn = gn_ref[...]

    rt = lax.broadcasted_iota(jnp.int32, (c, c), 0)
    ct = lax.broadcasted_iota(jnp.int32, (c, c), 1)
    tril = jnp.where(ct <= rt, 1.0, 0.0).astype(BF16)
    row = lax.broadcasted_iota(jnp.int32, (c, HG_DK), 0)
    qk_scale = HG_DK ** -0.5

    def chunk_body(ci, carry):
        r0 = pl.multiple_of(ci * c, c)
        for h in range(HG_HEADS):
            def seg(k):
                return h4_ref[0, pl.ds(r0, c), pl.ds(k * HG_W + h * HG_DK, HG_DK)]
            hq, hf, hi, hg = seg(0), seg(1), seg(2), seg(3)
            lb = lb_all[:, h * HG_DK:(h + 1) * HG_DK]
            f = lb + (1.0 - lb) * _sigmoid(hf)
            logf = jnp.log(f)
            kk = 1.0 - f
            qq = hq * _sigmoid(hq) * qk_scale
            vb = hi.astype(BF16)

            p0 = logf.astype(BF16)
            r1 = logf - p0.astype(F32)
            p1 = r1.astype(BF16)
            p2 = (r1 - p1.astype(F32)).astype(BF16)
            b = _dot(tril, p0) + _dot(tril, p1) + _dot(tril, p2)
            bend = b[c - 1:c, :]

            st = st_ref[h]
            o = _dot_nt((qq * jnp.exp(b)).astype(BF16), st.astype(BF16))
            kd = (kk * jnp.exp(bend - b)).astype(BF16)
            st_ref[h] = st * jnp.exp(bend) + lax.dot_general(vb, kd, TN_DIMS, preferred_element_type=F32)

            a = jnp.zeros((c, c), F32)
            n = c // 2
            while n >= DIAG_BLOCK:
                qparts, kparts = [], []
                for j in range(c // (2 * n)):
                    lo, mid, hi_ = 2 * n * j, 2 * n * j + n, 2 * n * (j + 1)
                    ref = b[mid - 1:mid, :]
                    qparts += [jnp.zeros((n, HG_DK), F32), qq[mid:hi_] * jnp.exp(b[mid:hi_] - ref)]
                    kparts += [kk[lo:mid] * jnp.exp(ref - b[lo:mid]), jnp.zeros((n, HG_DK), F32)]
                an = _dot_nt(jnp.concatenate(qparts, axis=0).astype(BF16),
                             jnp.concatenate(kparts, axis=0).astype(BF16))
                a = a + jnp.where((rt // (2 * n)) == (ct // (2 * n)), an, 0.0)
                n //= 2
            o = o + _dot(a.astype(BF16), vb)

            for d in range(DIAG_BLOCK):
                if d == 0:
                    w = qq * kk
                    vd = hi
                else:
                    arg = jnp.where((row % DIAG_BLOCK) >= d, b - pltpu.roll(b, d, 0), -jnp.inf)
                    w = qq * pltpu.roll(kk, d, 0) * jnp.exp(arg)
                    vd = pltpu.roll(hi, d, 0)
                o = o + jnp.sum(w, axis=1, keepdims=True) * vd

            ob = _rms(o, gn) * (hg * _sigmoid(hg))
            o_ref[0, pl.ds(r0, c), pl.ds(h * HG_DV, HG_DV)] = ob.astype(BF16)
        return carry

    lax.fori_loop(0, nchunk, chunk_body, 0)

    @pl.when(t == pl.num_programs(1) - 1)
    def _():
        for h in range(HG_HEADS):
            sout_ref[0, h] = st_ref[h].T


def _hgrn(h4, lb_logits, gnorm, s0, layer, tt):
    B, L, _ = h4.shape
    c = min(CHUNK, L)
    tt = min(tt, L)
    full = lambda shape: pl.BlockSpec(shape, lambda b, t: (0,) * len(shape))
    return pl.pallas_call(
        functools.partial(_hgrn_kernel, layer=layer, c=c, nchunk=tt // c),
        out_shape=(jax.ShapeDtypeStruct((B, L, HG_HEADS * HG_DV), BF16),
                   jax.ShapeDtypeStruct((B, HG_HEADS, HG_DK, HG_DV), F32)),
        grid=(B, L // tt),
        in_specs=[
            pl.BlockSpec((1, tt, 4 * HG_W), lambda b, t: (b, t, 0)),
            full(lb_logits.shape), full(gnorm.shape),
            pl.BlockSpec((1, HG_HEADS, HG_DK, HG_DV), lambda b, t: (b, 0, 0, 0)),
        ],
        out_specs=(pl.BlockSpec((1, tt, HG_HEADS * HG_DV), lambda b, t: (b, t, 0)),
                   pl.BlockSpec((1, HG_HEADS, HG_DK, HG_DV), lambda b, t: (b, 0, 0, 0))),
        scratch_shapes=[pltpu.VMEM((HG_HEADS, HG_DV, HG_DK), F32)],
        compiler_params=_params("parallel", "arbitrary"), name="hgrn",
    )(h4, lb_logits, gnorm, s0)


def _merge_kernel(x_ref, oa_ref, ob_ref, g_ref, wpa_ref, wpb_ref, wout_ref, y_ref):
    ga = g_ref[0, :, :D_MODEL]
    gb = g_ref[0, :, D_MODEL:]
    mix = ga * _dot(oa_ref[0], wpa_ref[...]) + gb * _dot(ob_ref[0], wpb_ref[...])
    y_ref[0] = x_ref[0] + _dot(mix.astype(BF16), wout_ref[...])


def _merge(x, oa, ob, g, wpa, wpb, wout, tm):
    B, L, _ = x.shape
    full = lambda shape: pl.BlockSpec(shape, lambda b, t: (0,) * len(shape))
    row = lambda w: pl.BlockSpec((1, tm, w), lambda b, t: (b, t, 0))
    return pl.pallas_call(
        _merge_kernel,
        out_shape=jax.ShapeDtypeStruct(x.shape, F32),
        grid=(B, L // tm),
        in_specs=[row(D_MODEL), row(oa.shape[2]), row(ob.shape[2]), row(2 * D_MODEL),
                  full(wpa.shape), full(wpb.shape), full(wout.shape)],
        out_specs=row(D_MODEL),
        compiler_params=_params("parallel", "parallel"), name="merge",
    )(x, oa, ob, g, wpa, wpb, wout)


FFN_SPLIT = 2
FFN_COLS = FFN_DIM // FFN_SPLIT


def _gelu_tanh(x):
    return 0.5 * x * (1.0 + jnp.tanh(math.sqrt(2.0 / math.pi) * (x + 0.044715 * (x * x * x))))


def _ffn_kernel(x_ref, cs_ref, nf_ref, wup_ref, cw_ref, cb_ref, wdn_ref, nfin_ref, y_ref, cnew_ref, carry_ref,
                *, final, tm):
    t = pl.program_id(1)

    @pl.when(t == 0)
    def _():
        carry_ref[0:CONV_W - 1, :] = cs_ref[0]

    x = x_ref[0]
    hb = _rms(x, nf_ref[...]).astype(BF16)
    row = lax.broadcasted_iota(jnp.int32, (tm, FFN_COLS), 0)
    acc = x
    for c in range(FFN_SPLIT):
        cs = slice(c * FFN_COLS, (c + 1) * FFN_COLS)
        a = _dot(hb, wup_ref[:, c * FFN_COLS:(c + 1) * FFN_COLS])
        val = _dot(hb, wup_ref[:, FFN_DIM + c * FFN_COLS:FFN_DIM + (c + 1) * FFN_COLS])
        prev2 = carry_ref[0:1, cs]
        prev1 = carry_ref[1:2, cs]
        a1 = jnp.where(row == 0, prev1, pltpu.roll(a, 1, 0))
        a2 = jnp.where(row == 0, prev2, jnp.where(row == 1, prev1, pltpu.roll(a, 2, 0)))
        conv = cb_ref[:, cs] + cw_ref[0:1, cs] * a2 + cw_ref[1:2, cs] * a1 + cw_ref[2:3, cs] * a
        gated = (_gelu_tanh(conv) * val).astype(BF16)
        acc = acc + _dot(gated, wdn_ref[c * FFN_COLS:(c + 1) * FFN_COLS, :])
        carry_ref[0:CONV_W - 1, cs] = a[tm - (CONV_W - 1):tm, :]
    if final:
        acc = _rms(acc, nfin_ref[...])
    y_ref[0] = acc

    @pl.when(t == pl.num_programs(1) - 1)
    def _():
        cnew_ref[0] = carry_ref[0:CONV_W - 1, :]


def _ffn(x, conv_state, nf, wup, cw, cb, wdn, nfin, final, tm):
    B, L, _ = x.shape
    full = lambda shape: pl.BlockSpec(shape, lambda b, t: (0,) * len(shape))
    return pl.pallas_call(
        functools.partial(_ffn_kernel, final=final, tm=tm),
        out_shape=(jax.ShapeDtypeStruct(x.shape, F32),
                   jax.ShapeDtypeStruct((B, CONV_W - 1, FFN_DIM), F32)),
        grid=(B, L // tm),
        in_specs=[
            pl.BlockSpec((1, tm, D_MODEL), lambda b, t: (b, t, 0)),
            pl.BlockSpec((1, CONV_W - 1, FFN_DIM), lambda b, t: (b, 0, 0)),
            full(nf.shape), full(wup.shape), full(cw.shape), full(cb.shape), full(wdn.shape), full(nfin.shape),
        ],
        out_specs=(pl.BlockSpec((1, tm, D_MODEL), lambda b, t: (b, t, 0)),
                   pl.BlockSpec((1, CONV_W - 1, FFN_DIM), lambda b, t: (b, 0, 0))),
        scratch_shapes=[pltpu.VMEM((SUBLANES, FFN_DIM), F32)],
        compiler_params=_params("parallel", "arbitrary"), name="ffn",
    )(x, conv_state, nf, wup, cw, cb, wdn, nfin)


def _rope_tables(pos):
    half = MLA_ROPE // 2
    inv = ROPE_BASE ** (-jnp.arange(half, dtype=F32) / half)
    ang = pos.astype(F32)[:, None] * inv[None, :]
    cos, sin = jnp.cos(ang), jnp.sin(ang)
    return jnp.tile(cos, (1, 4)), jnp.concatenate([-sin, sin, -sin, sin], axis=1)


def _prep_layer(w_in_l, w_uq_l, w_ukv_l):
    kr = w_in_l[:, OFF_KR:OFF_KR + MLA_ROPE]
    win = jnp.concatenate([w_in_l[:, :OFF_KR], kr, kr, w_in_l[:, OFF_KR + MLA_ROPE:]], axis=1).astype(BF16)
    uq = w_uq_l.reshape(MLA_Q_RANK, MLA_HEADS, MLA_QK)
    wuq = jnp.concatenate([uq[:, :, :MLA_NOPE].reshape(MLA_Q_RANK, -1),
                           uq[:, :, MLA_NOPE:].reshape(MLA_Q_RANK, -1)], axis=1).astype(BF16)
    ukv = w_ukv_l.reshape(MLA_KV_RANK, MLA_HEADS, MLA_NOPE + MLA_V)
    wuk = ukv[:, :, :MLA_NOPE].reshape(MLA_KV_RANK, -1).astype(BF16)
    wuv = ukv[:, :, MLA_NOPE:].reshape(MLA_KV_RANK, -1).astype(BF16)
    return win, wuq, wuk, wuv


def _tile(L, pref):
    return pref if L % pref == 0 else L


def kernel(x_prompt, x_sample, cache_mla_latent, cache_mla_krope, state_hgrn, state_ffn_conv, norm_mix, w_in, q_norm, w_uq, kv_norm, w_ukv, lb_logits, hgrn_norm, w_proj_a, w_proj_b, w_out, norm_ffn, w_up, conv_w, conv_b, w_down, norm_final):
    depth = w_in.shape[0]
    B, Lp, _ = x_prompt.shape
    Bs, Ls, _ = x_sample.shape
    past = cache_mla_latent.shape[2]

    cos_p, sin_p = _rope_tables(jnp.arange(Lp))
    cos_s, sin_s = _rope_tables(past + jnp.arange(Ls))
    s0_p = jnp.zeros((B, HG_HEADS, HG_DK, HG_DV), F32)
    conv0_p = jnp.zeros((B, CONV_W - 1, FFN_DIM), F32)
    nfin = norm_final.reshape(1, D_MODEL)
    lbl = lb_logits.astype(F32)

    tm_p, tm_s = _tile(Lp, 256), _tile(Ls, 256)
    tq = _tile(Lp, 512)

    xp, xs = x_prompt, x_sample
    outs_p, outs_s = [], []
    for l in range(depth):
        win, wuq, wuk, wuv = _prep_layer(w_in[l], w_uq[l], w_ukv[l])
        wuvt = wuv.T
        nmix = norm_mix[l].reshape(1, -1)
        qn = q_norm[l].reshape(1, -1)
        kvn = kv_norm[l].reshape(1, -1)
        gn = hgrn_norm[l].reshape(1, -1)
        nf = norm_ffn[l].reshape(1, -1)
        wpa, wpb, wo = w_proj_a[l].astype(BF16), w_proj_b[l].astype(BF16), w_out[l].astype(BF16)
        wup, wdn = w_up[l].astype(BF16), w_down[l].astype(BF16)
        cw, cb = conv_w[l], conv_b[l].reshape(1, -1)
        final = l == depth - 1

        q, k, vt, lat, kpe, h4, g = _inproj(xp, cos_p, sin_p, nmix, win, qn, wuq, kvn, wuk, wuvt, tm_p)
        oa = _attn_prompt(q, k, vt, tq)
        ob, s_new = _hgrn(h4, lbl, gn, s0_p, l, 512)
        x1 = _merge(xp, oa, ob, g, wpa, wpb, wo, tm_p)
        xp, cnew = _ffn(x1, conv0_p, nf, wup, cw, cb, wdn, nfin, final, tm_p)
        outs_p.append((lat, kpe, s_new, cnew))

        q, k, vt, lat, kpe, h4, g = _inproj(xs, cos_s, sin_s, nmix, win, qn, wuq, kvn, wuk, wuvt, tm_s)
        oa = _attn_sample(q, k, lat, cache_mla_latent[l], cache_mla_krope[l], wuk, wuv)
        ob, s_new = _hgrn(h4, lbl, gn, state_hgrn[l], l, 512)
        x1 = _merge(xs, oa, ob, g, wpa, wpb, wo, tm_s)
        xs, cnew = _ffn(x1, state_ffn_conv[l], nf, wup, cw, cb, wdn, nfin, final, tm_s)
        outs_s.append((lat, kpe, s_new, cnew))

    stack = lambda outs, i: jnp.stack([o[i] for o in outs])
    return (xp, xs,
            stack(outs_p, 0), stack(outs_p, 1), stack(outs_p, 2), stack(outs_p, 3),
            stack(outs_s, 0), stack(outs_s, 1), stack(outs_s, 2), stack(outs_s, 3))
```

```python
import functools
import math

import numpy as np
import jax
import jax.numpy as jnp
from jax import lax
from jax.experimental import pallas as pl
from jax.experimental.pallas import tpu as pltpu

D_MODEL = 1024
CHUNK = 64
EPS = 1e-6
MLA_HEADS = 8
MLA_Q_RANK = 384
MLA_KV_RANK = 256
MLA_NOPE = 128
MLA_ROPE = 64
MLA_V = 128
ROPE_BASE = 10000.0
HG_HEADS = 4
HG_DK = 128
HG_DV = 128
FFN_DIM = 2816
CONV_W = 3

MLA_QK = MLA_NOPE + MLA_ROPE
HG_W = HG_HEADS * HG_DK
QK_PAD = 2 * MLA_NOPE
VT_ROWS = MLA_V + 16
LANES = 128
SUBLANES = 8
VMEM_LIMIT = 56 * 1024 * 1024
ROW_TILE = 512

OFF_CQ = 0
OFF_CKV = OFF_CQ + MLA_Q_RANK
OFF_KR = OFF_CKV + MLA_KV_RANK
OFF_HG = OFF_KR + 2 * MLA_ROPE
OFF_GATE = OFF_HG + 4 * HG_W
IN_COLS_R = OFF_GATE + 2 * D_MODEL

BF16 = jnp.bfloat16
F32 = jnp.float32
NT_DIMS = (((1,), (1,)), ((), ()))
TN_DIMS = (((0,), (0,)), ((), ()))


def _dot(a, b):
    return jnp.dot(a, b, preferred_element_type=F32)


def _dot_nt(a, b):
    return lax.dot_general(a, b, NT_DIMS, preferred_element_type=F32)


def _rms(x, g):
    return x * lax.rsqrt(jnp.mean(x * x, axis=-1, keepdims=True) + EPS) * g


def _sigmoid(x):
    return 0.5 * (jnp.tanh(0.5 * x) + 1.0)


def _resident(shape):
    return pl.BlockSpec(shape, lambda *_: (0,) * len(shape), pipeline_mode=pl.Buffered(1))


def _params(*sem):
    return pltpu.CompilerParams(dimension_semantics=sem, vmem_limit_bytes=VMEM_LIMIT)


def _inproj_kernel(x_ref, cos_ref, sin_ref, nmix_ref, win_ref, qn_ref, wuq_ref, kvn_ref, wuk_ref, wuvt_ref,
                   q_ref, k_ref, vt_ref, lat_ref, kpe_ref, h4_ref, g_ref, *, qscale):
    x = x_ref[0]
    hb = _rms(x, nmix_ref[...]).astype(BF16)
    cosf = cos_ref[...]
    sinf = sin_ref[...]
    lane = lax.broadcasted_iota(jnp.int32, cosf.shape, 1)
    first_half = (lane % MLA_ROPE) < (MLA_ROPE // 2)
    low_head = lane < MLA_ROPE

    def rot(v):
        sw = jnp.where(first_half, pltpu.roll(v, LANES - MLA_ROPE // 2, 1), pltpu.roll(v, MLA_ROPE // 2, 1))
        return v * cosf + sw * sinf

    cq = _dot(hb, win_ref[:, OFF_CQ:OFF_CKV])
    cqn = _rms(cq, qn_ref[...]).astype(BF16)
    q = _dot(cqn, wuq_ref[...]) * qscale

    ckv = _dot(hb, win_ref[:, OFF_CKV:OFF_KR])
    lat = _rms(ckv, kvn_ref[...])
    lat_ref[0] = lat
    latb = lat.astype(BF16)
    kn = _dot(latb, wuk_ref[...])
    vt = _dot_nt(wuvt_ref[...], latb).astype(BF16)
    ones = jnp.ones((VT_ROWS - MLA_V, vt.shape[1]), BF16)
    for h in range(MLA_HEADS):
        vt_ref[0, h * VT_ROWS:h * VT_ROWS + MLA_V, :] = vt[h * MLA_V:(h + 1) * MLA_V]
        vt_ref[0, h * VT_ROWS + MLA_V:(h + 1) * VT_ROWS, :] = ones

    kpe = rot(_dot(hb, win_ref[:, OFF_KR:OFF_HG]))
    kpe_ref[0] = kpe[:, :MLA_ROPE]
    kpeb = kpe.astype(BF16)

    nope_w = MLA_HEADS * MLA_NOPE
    for g in range(MLA_HEADS // 2):
        r = rot(q[:, nope_w + g * LANES: nope_w + (g + 1) * LANES])
        for hh in range(2):
            h = 2 * g + hh
            rh = jnp.where(low_head if hh == 0 else jnp.logical_not(low_head), r, 0.0)
            qh = jnp.concatenate([q[:, h * MLA_NOPE:(h + 1) * MLA_NOPE], rh], axis=1)
            q_ref[0, h] = qh.astype(BF16)
            k_ref[0, h] = jnp.concatenate([kn[:, h * MLA_NOPE:(h + 1) * MLA_NOPE].astype(BF16), kpeb], axis=1)

    h4_ref[0] = _dot(hb, win_ref[:, OFF_HG:OFF_GATE])
    g_ref[0] = _sigmoid(_dot(hb, win_ref[:, OFF_GATE:IN_COLS_R]))


def _inproj(x, cosf, sinf, nmix, win, qn, wuq, kvn, wuk, wuvt, tm):
    B, L, _ = x.shape
    nt = L // tm
    H = MLA_HEADS
    qscale = (1.0 / math.sqrt(MLA_QK)) * math.log2(math.e)
    full = _resident
    out_shape = (
        jax.ShapeDtypeStruct((B, H, L, QK_PAD), BF16),
        jax.ShapeDtypeStruct((B, H, L, QK_PAD), BF16),
        jax.ShapeDtypeStruct((B, H * VT_ROWS, L), BF16),
        jax.ShapeDtypeStruct((B, L, MLA_KV_RANK), F32),
        jax.ShapeDtypeStruct((B, L, MLA_ROPE), F32),
        jax.ShapeDtypeStruct((B, L, 4 * HG_W), F32),
        jax.ShapeDtypeStruct((B, L, 2 * D_MODEL), F32),
    )
    out_specs = (
        pl.BlockSpec((1, H, tm, QK_PAD), lambda b, t: (b, 0, t, 0)),
        pl.BlockSpec((1, H, tm, QK_PAD), lambda b, t: (b, 0, t, 0)),
        pl.BlockSpec((1, H * VT_ROWS, tm), lambda b, t: (b, 0, t)),
        pl.BlockSpec((1, tm, MLA_KV_RANK), lambda b, t: (b, t, 0)),
        pl.BlockSpec((1, tm, MLA_ROPE), lambda b, t: (b, t, 0)),
        pl.BlockSpec((1, tm, 4 * HG_W), lambda b, t: (b, t, 0)),
        pl.BlockSpec((1, tm, 2 * D_MODEL), lambda b, t: (b, t, 0)),
    )
    in_specs = [
        pl.BlockSpec((1, tm, D_MODEL), lambda b, t: (b, t, 0)),
        pl.BlockSpec((tm, LANES), lambda b, t: (t, 0)),
        pl.BlockSpec((tm, LANES), lambda b, t: (t, 0)),
        full(nmix.shape), full(win.shape), full(qn.shape), full(wuq.shape), full(kvn.shape),
        full(wuk.shape), full(wuvt.shape),
    ]
    return pl.pallas_call(
        functools.partial(_inproj_kernel, qscale=qscale),
        out_shape=out_shape, grid=(B, nt), in_specs=in_specs, out_specs=out_specs,
        compiler_params=_params("parallel", "arbitrary"), name="inproj",
    )(x, cosf, sinf, nmix, win, qn, wuq, kvn, wuk, wuvt)


ATTN_HEAD_GROUP = 2
ATTN_TQ = 2048
ATTN_TK = 512


def _attn_kernel(q_ref, k_ref, vt_ref, o_ref, acc_ref, *, tq, tk):
    i = pl.program_id(2)
    G = ATTN_HEAD_GROUP
    r = tq // tk

    def step(j, ms, q0):
        diagonal = q0 is not None
        q0 = q0 or 0
        off = pl.multiple_of(j * tk, tk)
        out = []
        for g in range(G):
            m = ms[g][:, q0:]
            kb = k_ref[0, g, pl.ds(off, tk), :]
            s = _dot_nt(kb, q_ref[0, g, q0:, :])
            if diagonal:
                kc = (q0 + lax.broadcasted_iota(jnp.int32, s.shape, 0)) // CHUNK
                qc = (q0 + lax.broadcasted_iota(jnp.int32, s.shape, 1)) // CHUNK
                s = jnp.where(kc <= qc, s, -jnp.inf)
            m_new = jnp.maximum(m, jnp.max(s, axis=0, keepdims=True))
            alpha = jnp.exp2(m - m_new)
            p = jnp.exp2(s - m_new).astype(BF16)
            vb = vt_ref[0, g * VT_ROWS:(g + 1) * VT_ROWS, pl.ds(off, tk)]
            acc_ref[g, :, q0:] = alpha * acc_ref[g, :, q0:] + _dot(vb, p)
            out.append(m_new if q0 == 0 else jnp.concatenate([ms[g][:, :q0], m_new], axis=1))
        return tuple(out)

    acc_ref[...] = jnp.zeros_like(acc_ref)
    ms = tuple(jnp.full((1, tq), -jnp.inf, F32) for _ in range(G))
    ms = lax.fori_loop(0, i * r, lambda j, c: step(j, c, None), ms)
    for d in range(r):
        ms = step(i * r + d, ms, d * tk)
    for g in range(G):
        acc = acc_ref[g]
        o_ref[0, :, g * MLA_V:(g + 1) * MLA_V] = (acc[:MLA_V] / acc[MLA_V:MLA_V + 1]).T.astype(BF16)


def _attn_prompt(q, k, vt, tq, tk):
    B, H, L, _ = q.shape
    G = ATTN_HEAD_GROUP
    return pl.pallas_call(
        functools.partial(_attn_kernel, tq=tq, tk=tk),
        out_shape=jax.ShapeDtypeStruct((B, L, H * MLA_V), BF16),
        grid=(B, H // G, L // tq),
        in_specs=[
            pl.BlockSpec((1, G, tq, QK_PAD), lambda b, h, i: (b, h, i, 0)),
            pl.BlockSpec((1, G, L, QK_PAD), lambda b, h, i: (b, h, 0, 0)),
            pl.BlockSpec((1, G * VT_ROWS, L), lambda b, h, i: (b, h, 0)),
        ],
        out_specs=pl.BlockSpec((1, tq, G * MLA_V), lambda b, h, i: (b, i, h)),
        scratch_shapes=[pltpu.VMEM((G, VT_ROWS, tq), F32)],
        compiler_params=_params("parallel", "parallel", "arbitrary"), name="attn_prompt",
    )(q, k, vt)


def _attn_sample_kernel(q_ref, kn_ref, latn_ref, latp_ref, kpep_ref, wuk_ref, wuv_ref, o_ref):
    latp = latp_ref[0].astype(BF16)
    latn = latn_ref[0].astype(BF16)
    knp = _dot(latp, wuk_ref[...]).astype(BF16)
    vp = _dot(latp, wuv_ref[...]).astype(BF16)
    vn = _dot(latn, wuv_ref[...]).astype(BF16)
    kp = kpep_ref[0].astype(BF16)
    kpd = jnp.concatenate([kp, kp], axis=1)
    for h in range(MLA_HEADS):
        sl = slice(h * MLA_NOPE, (h + 1) * MLA_NOPE)
        qh = q_ref[0, h]
        khp = jnp.concatenate([knp[:, sl], kpd], axis=1)
        s1 = _dot_nt(qh, khp)
        s2 = _dot_nt(qh, kn_ref[0, h])
        m = jnp.maximum(jnp.max(s1, axis=1, keepdims=True), jnp.max(s2, axis=1, keepdims=True))
        p1 = jnp.exp2(s1 - m)
        p2 = jnp.exp2(s2 - m)
        l = jnp.sum(p1, axis=1, keepdims=True) + jnp.sum(p2, axis=1, keepdims=True)
        o = _dot(p1.astype(BF16), vp[:, sl]) + _dot(p2.astype(BF16), vn[:, sl])
        o_ref[0, :, sl] = (o / l).astype(BF16)


def _attn_sample(q, kn, latn, latp, kpep, wuk, wuv):
    B, H, Ls, _ = q.shape
    P = latp.shape[1]
    full = _resident
    return pl.pallas_call(
        _attn_sample_kernel,
        out_shape=jax.ShapeDtypeStruct((B, Ls, H * MLA_V), BF16),
        grid=(B,),
        in_specs=[
            pl.BlockSpec((1, H, Ls, QK_PAD), lambda b: (b, 0, 0, 0)),
            pl.BlockSpec((1, H, Ls, QK_PAD), lambda b: (b, 0, 0, 0)),
            pl.BlockSpec((1, Ls, MLA_KV_RANK), lambda b: (b, 0, 0)),
            pl.BlockSpec((1, P, MLA_KV_RANK), lambda b: (b, 0, 0)),
            pl.BlockSpec((1, P, MLA_ROPE), lambda b: (b, 0, 0)),
            full(wuk.shape), full(wuv.shape),
        ],
        out_specs=pl.BlockSpec((1, Ls, H * MLA_V), lambda b: (b, 0, 0)),
        compiler_params=_params("parallel"), name="attn_sample",
    )(q, kn, latn, latp, kpep, wuk, wuv)


DIAG_BLOCK = SUBLANES
HGRN_UNROLL = 4


def _hgrn_kernel(h4_ref, lbl_ref, gn_ref, s0_ref, o_ref, sout_ref, st_ref, pb_ref, ob_ref, *, layer, c, nchunk, unroll):
    t = pl.program_id(1)

    @pl.when(t == 0)
    def _():
        for h in range(HG_HEADS):
            st_ref[h] = s0_ref[0, h].T

    lg = lbl_ref[...]
    e = jnp.exp(lg - jnp.max(lg, axis=0, keepdims=True))
    sm = e / jnp.sum(e, axis=0, keepdims=True)
    lb_all = sm[0:1] * 0.0
    for i in range(1, layer + 1):
        lb_all = lb_all + sm[i:i + 1]
    gn = gn_ref[...]

    rt = lax.broadcasted_iota(jnp.int32, (c, c), 0)
    ct = lax.broadcasted_iota(jnp.int32, (c, c), 1)
    tril = jnp.where(ct <= rt, 1.0, 0.0).astype(BF16)
    nblk = c // DIAG_BLOCK
    qk_scale = HG_DK ** -0.5

    def chunk(ci, u):
        r0 = pl.multiple_of(ci * c, c)
        for h in range(HG_HEADS):
            sl = u * HG_HEADS + h
            def seg(k):
                return h4_ref[0, pl.ds(r0, c), pl.ds(k * HG_W + h * HG_DK, HG_DK)]
            hq, hf, hi, hg = seg(0), seg(1), seg(2), seg(3)
            lb = lb_all[:, h * HG_DK:(h + 1) * HG_DK]
            f = lb + (1.0 - lb) * _sigmoid(hf)
            logf = jnp.log(f)
            kk = 1.0 - f
            qq = hq * _sigmoid(hq) * qk_scale
            vb = hi.astype(BF16)

            p0 = logf.astype(BF16)
            r1 = logf - p0.astype(F32)
            p1 = r1.astype(BF16)
            p2 = (r1 - p1.astype(F32)).astype(BF16)
            b = _dot(tril, p0) + _dot(tril, p1) + _dot(tril, p2)
            bend = b[c - 1:c, :]

            st = st_ref[h]
            o = _dot_nt((qq * jnp.exp(b)).astype(BF16), st.astype(BF16))
            kd = (kk * jnp.exp(bend - b)).astype(BF16)
            st_ref[h] = st * jnp.exp(bend) + lax.dot_general(vb, kd, TN_DIMS, preferred_element_type=F32)

            a = jnp.zeros((c, c), F32)
            n = c // 2
            while n >= DIAG_BLOCK:
                qparts, kparts = [], []
                for j in range(c // (2 * n)):
                    lo, mid, hi_ = 2 * n * j, 2 * n * j + n, 2 * n * (j + 1)
                    ref = b[mid - 1:mid, :]
                    qparts += [jnp.zeros((n, HG_DK), F32), qq[mid:hi_] * jnp.exp(b[mid:hi_] - ref)]
                    kparts += [kk[lo:mid] * jnp.exp(ref - b[lo:mid]), jnp.zeros((n, HG_DK), F32)]
                an = _dot_nt(jnp.concatenate(qparts, axis=0).astype(BF16),
                             jnp.concatenate(kparts, axis=0).astype(BF16))
                a = a + jnp.where((rt // (2 * n)) == (ct // (2 * n)), an, 0.0)
                n //= 2
            o = o + _dot(a.astype(BF16), vb)

            pb_ref[sl, 0] = qq
            pb_ref[sl, 1] = kk
            pb_ref[sl, 2] = b
            pb_ref[sl, 3] = hi
            rows = [[pb_ref[sl, k, pl.ds(a, nblk, stride=DIAG_BLOCK), :] for a in range(DIAG_BLOCK)] for k in range(4)]
            for a in range(DIAG_BLOCK):
                oa = jnp.zeros((nblk, HG_DV), F32)
                for e_ in range(a + 1):
                    w = rows[0][a] * rows[1][e_]
                    if e_ < a:
                        w = w * jnp.exp(rows[2][a] - rows[2][e_])
                    oa = oa + jnp.sum(w, axis=1, keepdims=True) * rows[3][e_]
                ob_ref[sl, pl.ds(a, nblk, stride=DIAG_BLOCK), :] = oa
            o = o + ob_ref[sl]

            ob = _rms(o, gn) * (hg * _sigmoid(hg))
            o_ref[0, pl.ds(r0, c), pl.ds(h * HG_DV, HG_DV)] = ob.astype(BF16)

    def unrolled(cu, carry):
        for u in range(unroll):
            chunk(cu * unroll + u, u)
        return carry

    lax.fori_loop(0, nchunk // unroll, unrolled, 0)

    @pl.when(t == pl.num_programs(1) - 1)
    def _():
        for h in range(HG_HEADS):
            sout_ref[0, h] = st_ref[h].T


def _hgrn(h4, lb_logits, gnorm, s0, layer, tt):
    B, L, _ = h4.shape
    c = min(CHUNK, L)
    tt = min(tt, L)
    nchunk = tt // c
    unroll = HGRN_UNROLL if nchunk % HGRN_UNROLL == 0 else 1
    full = _resident
    return pl.pallas_call(
        functools.partial(_hgrn_kernel, layer=layer, c=c, nchunk=nchunk, unroll=unroll),
        out_shape=(jax.ShapeDtypeStruct((B, L, HG_HEADS * HG_DV), BF16),
                   jax.ShapeDtypeStruct((B, HG_HEADS, HG_DK, HG_DV), F32)),
        grid=(B, L // tt),
        in_specs=[
            pl.BlockSpec((1, tt, 4 * HG_W), lambda b, t: (b, t, 0)),
            full(lb_logits.shape), full(gnorm.shape),
            pl.BlockSpec((1, HG_HEADS, HG_DK, HG_DV), lambda b, t: (b, 0, 0, 0)),
        ],
        out_specs=(pl.BlockSpec((1, tt, HG_HEADS * HG_DV), lambda b, t: (b, t, 0)),
                   pl.BlockSpec((1, HG_HEADS, HG_DK, HG_DV), lambda b, t: (b, 0, 0, 0))),
        scratch_shapes=[pltpu.VMEM((HG_HEADS, HG_DV, HG_DK), F32),
                        pltpu.VMEM((unroll * HG_HEADS, 4, c, HG_DK), F32),
                        pltpu.VMEM((unroll * HG_HEADS, c, HG_DV), F32)],
        compiler_params=_params("parallel", "arbitrary"), name="hgrn",
    )(h4, lb_logits, gnorm, s0)


def _merge_kernel(x_ref, oa_ref, ob_ref, g_ref, wpa_ref, wpb_ref, wout_ref, y_ref):
    ga = g_ref[0, :, :D_MODEL]
    gb = g_ref[0, :, D_MODEL:]
    mix = ga * _dot(oa_ref[0], wpa_ref[...]) + gb * _dot(ob_ref[0], wpb_ref[...])
    y_ref[0] = x_ref[0] + _dot(mix.astype(BF16), wout_ref[...])


def _merge(x, oa, ob, g, wpa, wpb, wout, tm):
    B, L, _ = x.shape
    full = _resident
    row = lambda w: pl.BlockSpec((1, tm, w), lambda b, t: (b, t, 0))
    return pl.pallas_call(
        _merge_kernel,
        out_shape=jax.ShapeDtypeStruct(x.shape, F32),
        grid=(B, L // tm),
        in_specs=[row(D_MODEL), row(oa.shape[2]), row(ob.shape[2]), row(2 * D_MODEL),
                  full(wpa.shape), full(wpb.shape), full(wout.shape)],
        out_specs=row(D_MODEL),
        compiler_params=_params("parallel", "parallel"), name="merge",
    )(x, oa, ob, g, wpa, wpb, wout)


FFN_SPLIT = 2
FFN_COLS = FFN_DIM // FFN_SPLIT


def _gelu_tanh(x):
    return 0.5 * x * (1.0 + jnp.tanh(math.sqrt(2.0 / math.pi) * (x + 0.044715 * (x * x * x))))


def _ffn_kernel(x_ref, cs_ref, nf_ref, wup_ref, cw_ref, cb_ref, wdn_ref, nfin_ref, y_ref, cnew_ref, carry_ref,
                *, final, tm):
    t = pl.program_id(1)

    @pl.when(t == 0)
    def _():
        carry_ref[0:CONV_W - 1, :] = cs_ref[0]

    x = x_ref[0]
    hb = _rms(x, nf_ref[...]).astype(BF16)
    row = lax.broadcasted_iota(jnp.int32, (tm, FFN_COLS), 0)
    acc = x
    for c in range(FFN_SPLIT):
        cs = slice(c * FFN_COLS, (c + 1) * FFN_COLS)
        a = _dot(hb, wup_ref[:, c * FFN_COLS:(c + 1) * FFN_COLS])
        val = _dot(hb, wup_ref[:, FFN_DIM + c * FFN_COLS:FFN_DIM + (c + 1) * FFN_COLS])
        prev2 = carry_ref[0:1, cs]
        prev1 = carry_ref[1:2, cs]
        a1 = jnp.where(row == 0, prev1, pltpu.roll(a, 1, 0))
        a2 = jnp.where(row == 0, prev2, jnp.where(row == 1, prev1, pltpu.roll(a, 2, 0)))
        conv = cb_ref[:, cs] + cw_ref[0:1, cs] * a2 + cw_ref[1:2, cs] * a1 + cw_ref[2:3, cs] * a
        gated = (_gelu_tanh(conv) * val).astype(BF16)
        acc = acc + _dot(gated, wdn_ref[c * FFN_COLS:(c + 1) * FFN_COLS, :])
        carry_ref[0:CONV_W - 1, cs] = a[tm - (CONV_W - 1):tm, :]
    if final:
        acc = _rms(acc, nfin_ref[...])
    y_ref[0] = acc

    @pl.when(t == pl.num_programs(1) - 1)
    def _():
        cnew_ref[0] = carry_ref[0:CONV_W - 1, :]


def _ffn(x, conv_state, nf, wup, cw, cb, wdn, nfin, final, tm):
    B, L, _ = x.shape
    full = _resident
    return pl.pallas_call(
        functools.partial(_ffn_kernel, final=final, tm=tm),
        out_shape=(jax.ShapeDtypeStruct(x.shape, F32),
                   jax.ShapeDtypeStruct((B, CONV_W - 1, FFN_DIM), F32)),
        grid=(B, L // tm),
        in_specs=[
            pl.BlockSpec((1, tm, D_MODEL), lambda b, t: (b, t, 0)),
            pl.BlockSpec((1, CONV_W - 1, FFN_DIM), lambda b, t: (b, 0, 0)),
            full(nf.shape), full(wup.shape), full(cw.shape), full(cb.shape), full(wdn.shape), full(nfin.shape),
        ],
        out_specs=(pl.BlockSpec((1, tm, D_MODEL), lambda b, t: (b, t, 0)),
                   pl.BlockSpec((1, CONV_W - 1, FFN_DIM), lambda b, t: (b, 0, 0))),
        scratch_shapes=[pltpu.VMEM((SUBLANES, FFN_DIM), F32)],
        compiler_params=_params("parallel", "arbitrary"), name="ffn",
    )(x, conv_state, nf, wup, cw, cb, wdn, nfin)


def _rope_tables(pos):
    half = MLA_ROPE // 2
    inv = ROPE_BASE ** (-jnp.arange(half, dtype=F32) / half)
    ang = pos.astype(F32)[:, None] * inv[None, :]
    cos, sin = jnp.cos(ang), jnp.sin(ang)
    return jnp.tile(cos, (1, 4)), jnp.concatenate([-sin, sin, -sin, sin], axis=1)


def _prep_layer(w_in_l, w_uq_l, w_ukv_l):
    kr = w_in_l[:, OFF_KR:OFF_KR + MLA_ROPE]
    win = jnp.concatenate([w_in_l[:, :OFF_KR], kr, kr, w_in_l[:, OFF_KR + MLA_ROPE:]], axis=1).astype(BF16)
    uq = w_uq_l.reshape(MLA_Q_RANK, MLA_HEADS, MLA_QK)
    wuq = jnp.concatenate([uq[:, :, :MLA_NOPE].reshape(MLA_Q_RANK, -1),
                           uq[:, :, MLA_NOPE:].reshape(MLA_Q_RANK, -1)], axis=1).astype(BF16)
    ukv = w_ukv_l.reshape(MLA_KV_RANK, MLA_HEADS, MLA_NOPE + MLA_V)
    wuk = ukv[:, :, :MLA_NOPE].reshape(MLA_KV_RANK, -1).astype(BF16)
    wuv = ukv[:, :, MLA_NOPE:].reshape(MLA_KV_RANK, -1).astype(BF16)
    return win, wuq, wuk, wuv


def _tile(L, pref):
    return pref if L % pref == 0 else L


def kernel(x_prompt, x_sample, cache_mla_latent, cache_mla_krope, state_hgrn, state_ffn_conv, norm_mix, w_in, q_norm, w_uq, kv_norm, w_ukv, lb_logits, hgrn_norm, w_proj_a, w_proj_b, w_out, norm_ffn, w_up, conv_w, conv_b, w_down, norm_final):
    depth = w_in.shape[0]
    B, Lp, _ = x_prompt.shape
    Bs, Ls, _ = x_sample.shape
    past = cache_mla_latent.shape[2]

    cos_p, sin_p = _rope_tables(jnp.arange(Lp))
    cos_s, sin_s = _rope_tables(past + jnp.arange(Ls))
    s0_p = jnp.zeros((B, HG_HEADS, HG_DK, HG_DV), F32)
    conv0_p = jnp.zeros((B, CONV_W - 1, FFN_DIM), F32)
    nfin = norm_final.reshape(1, D_MODEL)
    lbl = lb_logits.astype(F32)

    tm_p, tm_s = _tile(Lp, ROW_TILE), _tile(Ls, ROW_TILE)
    tq = _tile(Lp, ATTN_TQ)
    tk = _tile(tq, ATTN_TK)

    xp, xs = x_prompt, x_sample
    outs_p, outs_s = [], []
    for l in range(depth):
        win, wuq, wuk, wuv = _prep_layer(w_in[l], w_uq[l], w_ukv[l])
        wuvt = wuv.T
        nmix = norm_mix[l].reshape(1, -1)
        qn = q_norm[l].reshape(1, -1)
        kvn = kv_norm[l].reshape(1, -1)
        gn = hgrn_norm[l].reshape(1, -1)
        nf = norm_ffn[l].reshape(1, -1)
        wpa, wpb, wo = w_proj_a[l].astype(BF16), w_proj_b[l].astype(BF16), w_out[l].astype(BF16)
        wup, wdn = w_up[l].astype(BF16), w_down[l].astype(BF16)
        cw, cb = conv_w[l], conv_b[l].reshape(1, -1)
        final = l == depth - 1

        q, k, vt, lat, kpe, h4, g = _inproj(xp, cos_p, sin_p, nmix, win, qn, wuq, kvn, wuk, wuvt, tm_p)
        oa = _attn_prompt(q, k, vt, tq, tk)
        ob, s_new = _hgrn(h4, lbl, gn, s0_p, l, 512)
        x1 = _merge(xp, oa, ob, g, wpa, wpb, wo, tm_p)
        xp, cnew = _ffn(x1, conv0_p, nf, wup, cw, cb, wdn, nfin, final, tm_p)
        outs_p.append((lat, kpe, s_new, cnew))

        q, k, vt, lat, kpe, h4, g = _inproj(xs, cos_s, sin_s, nmix, win, qn, wuq, kvn, wuk, wuvt, tm_s)
        oa = _attn_sample(q, k, lat, cache_mla_latent[l], cache_mla_krope[l], wuk, wuv)
        ob, s_new = _hgrn(h4, lbl, gn, state_hgrn[l], l, 512)
        x1 = _merge(xs, oa, ob, g, wpa, wpb, wo, tm_s)
        xs, cnew = _ffn(x1, state_ffn_conv[l], nf, wup, cw, cb, wdn, nfin, final, tm_s)
        outs_s.append((lat, kpe, s_new, cnew))

    stack = lambda outs, i: jnp.stack([o[i] for o in outs])
    return (xp, xs,
            stack(outs_p, 0), stack(outs_p, 1), stack(outs_p, 2), stack(outs_p, 3),
            stack(outs_s, 0), stack(outs_s, 1), stack(outs_s, 2), stack(outs_s, 3))
```

```python
import functools
import math

import numpy as np
import jax
import jax.numpy as jnp
from jax import lax
from jax.experimental import pallas as pl
from jax.experimental.pallas import tpu as pltpu

D_MODEL = 1024
CHUNK = 64
EPS = 1e-6
MLA_HEADS = 8
MLA_Q_RANK = 384
MLA_KV_RANK = 256
MLA_NOPE = 128
MLA_ROPE = 64
MLA_V = 128
ROPE_BASE = 10000.0
HG_HEADS = 4
HG_DK = 128
HG_DV = 128
FFN_DIM = 2816
CONV_W = 3

MLA_QK = MLA_NOPE + MLA_ROPE
HG_W = HG_HEADS * HG_DK
QK_PAD = 2 * MLA_NOPE
VT_ROWS = MLA_V + 16
LANES = 128
SUBLANES = 8
VMEM_LIMIT = 56 * 1024 * 1024
ROW_TILE = 512

OFF_CQ = 0
OFF_CKV = OFF_CQ + MLA_Q_RANK
OFF_KR = OFF_CKV + MLA_KV_RANK
OFF_HG = OFF_KR + 2 * MLA_ROPE
OFF_GATE = OFF_HG + 4 * HG_W
IN_COLS_R = OFF_GATE + 2 * D_MODEL

BF16 = jnp.bfloat16
F32 = jnp.float32
NT_DIMS = (((1,), (1,)), ((), ()))
TN_DIMS = (((0,), (0,)), ((), ()))


def _dot(a, b):
    return jnp.dot(a, b, preferred_element_type=F32)


def _dot_nt(a, b):
    return lax.dot_general(a, b, NT_DIMS, preferred_element_type=F32)


def _rms(x, g):
    return x * lax.rsqrt(jnp.mean(x * x, axis=-1, keepdims=True) + EPS) * g


def _sigmoid(x):
    return 0.5 * (jnp.tanh(0.5 * x) + 1.0)


def _resident(shape):
    return pl.BlockSpec(shape, lambda *_: (0,) * len(shape), pipeline_mode=pl.Buffered(1))


def _aliased():
    return pl.BlockSpec(memory_space=pl.ANY)


def _params(*sem):
    return pltpu.CompilerParams(dimension_semantics=sem, vmem_limit_bytes=VMEM_LIMIT)


def _inproj_kernel(x_ref, cos_ref, sin_ref, nmix_ref, win_ref, qn_ref, wuq_ref, kvn_ref, wuk_ref, wuvt_ref,
                   lat_in, kpe_in, *out_refs, qscale, sb, tm, with_vt):
    del lat_in, kpe_in
    if with_vt:
        q_ref, k_ref, vt_ref, lat_ref, kpe_ref, h4_ref, g_ref = out_refs
    else:
        q_ref, k_ref, lat_ref, kpe_ref, h4_ref, g_ref = out_refs
    rows = sb * tm
    x = x_ref[...].reshape(rows, D_MODEL)
    hb = _rms(x, nmix_ref[...]).astype(BF16)
    cosf = cos_ref[...]
    sinf = sin_ref[...]
    lane = lax.broadcasted_iota(jnp.int32, cosf.shape, 1)
    first_half = (lane % MLA_ROPE) < (MLA_ROPE // 2)
    low_head = lane < MLA_ROPE

    def rot(v):
        sw = jnp.where(first_half, pltpu.roll(v, LANES - MLA_ROPE // 2, 1), pltpu.roll(v, MLA_ROPE // 2, 1))
        return v * cosf + sw * sinf

    cq = _dot(hb, win_ref[:, OFF_CQ:OFF_CKV])
    cqn = _rms(cq, qn_ref[...]).astype(BF16)
    q = _dot(cqn, wuq_ref[...]) * qscale

    ckv = _dot(hb, win_ref[:, OFF_CKV:OFF_KR])
    lat = _rms(ckv, kvn_ref[...])
    lat_ref[0] = lat.reshape(sb, tm, MLA_KV_RANK)
    latb = lat.astype(BF16)
    kn = _dot(latb, wuk_ref[...])
    if with_vt:
        vt = _dot_nt(wuvt_ref[...], latb).astype(BF16)
        ones = jnp.ones((VT_ROWS - MLA_V, rows), BF16)
        for h in range(MLA_HEADS):
            vt_ref[0, h * VT_ROWS:h * VT_ROWS + MLA_V, :] = vt[h * MLA_V:(h + 1) * MLA_V]
            vt_ref[0, h * VT_ROWS + MLA_V:(h + 1) * VT_ROWS, :] = ones

    kpe = rot(_dot(hb, win_ref[:, OFF_KR:OFF_HG]))
    kpe_ref[0] = kpe[:, :MLA_ROPE].reshape(sb, tm, MLA_ROPE)
    kpeb = kpe.astype(BF16)

    nope_w = MLA_HEADS * MLA_NOPE
    for g in range(MLA_HEADS // 2):
        r = rot(q[:, nope_w + g * LANES: nope_w + (g + 1) * LANES])
        for hh in range(2):
            h = 2 * g + hh
            rh = jnp.where(low_head if hh == 0 else jnp.logical_not(low_head), r, 0.0)
            qh = jnp.concatenate([q[:, h * MLA_NOPE:(h + 1) * MLA_NOPE], rh], axis=1).astype(BF16)
            kh = jnp.concatenate([kn[:, h * MLA_NOPE:(h + 1) * MLA_NOPE].astype(BF16), kpeb], axis=1)
            q_ref[:, h] = qh.reshape(sb, tm, QK_PAD)
            k_ref[:, h] = kh.reshape(sb, tm, QK_PAD)

    h4_ref[...] = _dot(hb, win_ref[:, OFF_HG:OFF_GATE]).reshape(sb, tm, 4 * HG_W)
    gates = _sigmoid(_dot(hb, win_ref[:, OFF_GATE:IN_COLS_R]))
    g_ref[...] = gates.astype(BF16).reshape(sb, tm, 2 * D_MODEL)


def _inproj(x, cosf, sinf, nmix, win, qn, wuq, kvn, wuk, wuvt, lat_buf, kpe_buf, layer, sb, tm, with_vt):
    B, L, _ = x.shape
    H = MLA_HEADS
    qscale = (1.0 / math.sqrt(MLA_QK)) * math.log2(math.e)
    full = _resident
    seq = lambda w: pl.BlockSpec((sb, tm, w), lambda b, t: (b, t, 0))
    stacked = lambda w: pl.BlockSpec((1, sb, tm, w), lambda b, t: (layer, b, t, 0))
    qk_shape = jax.ShapeDtypeStruct((B, H, L, QK_PAD), BF16)
    qk_spec = pl.BlockSpec((sb, H, tm, QK_PAD), lambda b, t: (b, 0, t, 0))
    out_shape = [qk_shape, qk_shape]
    out_specs = [qk_spec, qk_spec]
    if with_vt:
        assert sb == 1
        out_shape.append(jax.ShapeDtypeStruct((B, H * VT_ROWS, L), BF16))
        out_specs.append(pl.BlockSpec((1, H * VT_ROWS, tm), lambda b, t: (b, 0, t)))
    lat_idx = len(out_shape)
    out_shape += [jax.ShapeDtypeStruct(lat_buf.shape, F32), jax.ShapeDtypeStruct(kpe_buf.shape, F32),
                  jax.ShapeDtypeStruct((B, L, 4 * HG_W), F32), jax.ShapeDtypeStruct((B, L, 2 * D_MODEL), BF16)]
    out_specs += [stacked(MLA_KV_RANK), stacked(MLA_ROPE), seq(4 * HG_W), seq(2 * D_MODEL)]
    in_specs = [
        seq(D_MODEL),
        pl.BlockSpec((sb * tm, LANES), lambda b, t: (t, 0)),
        pl.BlockSpec((sb * tm, LANES), lambda b, t: (t, 0)),
        full(nmix.shape), full(win.shape), full(qn.shape), full(wuq.shape), full(kvn.shape),
        full(wuk.shape), full(wuvt.shape),
        _aliased(), _aliased(),
    ]
    return pl.pallas_call(
        functools.partial(_inproj_kernel, qscale=qscale, sb=sb, tm=tm, with_vt=with_vt),
        out_shape=tuple(out_shape), grid=(B // sb, L // tm), in_specs=in_specs, out_specs=tuple(out_specs),
        input_output_aliases={10: lat_idx, 11: lat_idx + 1},
        compiler_params=_params("parallel", "arbitrary"), name="inproj",
    )(x, cosf, sinf, nmix, win, qn, wuq, kvn, wuk, wuvt, lat_buf, kpe_buf)


ATTN_HEAD_GROUP = 2
ATTN_TQ = 2048
ATTN_TK = 512


def _attn_kernel(q_ref, k_ref, vt_ref, o_ref, acc_ref, *, tq, tk):
    i = pl.program_id(2)
    G = ATTN_HEAD_GROUP
    r = tq // tk

    def step(j, ms, q0):
        diagonal = q0 is not None
        q0 = q0 or 0
        off = pl.multiple_of(j * tk, tk)
        out = []
        for g in range(G):
            m = ms[g][:, q0:]
            kb = k_ref[0, g, pl.ds(off, tk), :]
            s = _dot_nt(kb, q_ref[0, g, q0:, :])
            if diagonal:
                kc = lax.broadcasted_iota(jnp.int32, (tk, tk), 0) // CHUNK
                qc = lax.broadcasted_iota(jnp.int32, (tk, tk), 1) // CHUNK
                sm = jnp.where(kc <= qc, s[:, :tk], -jnp.inf)
                s = sm if s.shape[1] == tk else jnp.concatenate([sm, s[:, tk:]], axis=1)
            m_new = jnp.maximum(m, jnp.max(s, axis=0, keepdims=True))
            alpha = jnp.exp2(m - m_new)
            p = jnp.exp2(s - m_new).astype(BF16)
            vb = vt_ref[0, g * VT_ROWS:(g + 1) * VT_ROWS, pl.ds(off, tk)]
            acc_ref[g, :, q0:] = alpha * acc_ref[g, :, q0:] + _dot(vb, p)
            out.append(m_new if q0 == 0 else jnp.concatenate([ms[g][:, :q0], m_new], axis=1))
        return tuple(out)

    acc_ref[...] = jnp.zeros_like(acc_ref)
    ms = tuple(jnp.full((1, tq), -jnp.inf, F32) for _ in range(G))
    ms = lax.fori_loop(0, i * r, lambda j, c: step(j, c, None), ms)
    for d in range(r):
        ms = step(i * r + d, ms, d * tk)
    for g in range(G):
        acc = acc_ref[g]
        o_ref[0, :, g * MLA_V:(g + 1) * MLA_V] = (acc[:MLA_V] / acc[MLA_V:MLA_V + 1]).T.astype(BF16)


def _attn_prompt(q, k, vt, tq, tk):
    B, H, L, _ = q.shape
    G = ATTN_HEAD_GROUP
    return pl.pallas_call(
        functools.partial(_attn_kernel, tq=tq, tk=tk),
        out_shape=jax.ShapeDtypeStruct((B, L, H * MLA_V), BF16),
        grid=(B, H // G, L // tq),
        in_specs=[
            pl.BlockSpec((1, G, tq, QK_PAD), lambda b, h, i: (b, h, i, 0)),
            pl.BlockSpec((1, G, L, QK_PAD), lambda b, h, i: (b, h, 0, 0)),
            pl.BlockSpec((1, G * VT_ROWS, L), lambda b, h, i: (b, h, 0)),
        ],
        out_specs=pl.BlockSpec((1, tq, G * MLA_V), lambda b, h, i: (b, i, h)),
        scratch_shapes=[pltpu.VMEM((G, VT_ROWS, tq), F32)],
        compiler_params=_params("parallel", "parallel", "arbitrary"), name="attn_prompt",
    )(q, k, vt)


def _attn_sample_kernel(q_ref, kn_ref, latn_ref, latp_ref, kpep_ref, wuk_ref, wuv_ref, o_ref):
    latp = latp_ref[0].astype(BF16)
    latn = latn_ref[0, 0].astype(BF16)
    knp = _dot(latp, wuk_ref[...]).astype(BF16)
    vp = _dot(latp, wuv_ref[...]).astype(BF16)
    vn = _dot(latn, wuv_ref[...]).astype(BF16)
    kp = kpep_ref[0].astype(BF16)
    kpd = jnp.concatenate([kp, kp], axis=1)
    for h in range(MLA_HEADS):
        sl = slice(h * MLA_NOPE, (h + 1) * MLA_NOPE)
        qh = q_ref[0, h]
        khp = jnp.concatenate([knp[:, sl], kpd], axis=1)
        s1 = _dot_nt(qh, khp)
        s2 = _dot_nt(qh, kn_ref[0, h])
        m = jnp.maximum(jnp.max(s1, axis=1, keepdims=True), jnp.max(s2, axis=1, keepdims=True))
        p1 = jnp.exp2(s1 - m)
        p2 = jnp.exp2(s2 - m)
        l = jnp.sum(p1, axis=1, keepdims=True) + jnp.sum(p2, axis=1, keepdims=True)
        o = _dot(p1.astype(BF16), vp[:, sl]) + _dot(p2.astype(BF16), vn[:, sl])
        o_ref[0, :, sl] = (o / l).astype(BF16)


def _attn_sample(q, kn, lat_buf, layer, latp, kpep, wuk, wuv):
    B, H, Ls, _ = q.shape
    P = latp.shape[1]
    full = _resident
    return pl.pallas_call(
        _attn_sample_kernel,
        out_shape=jax.ShapeDtypeStruct((B, Ls, H * MLA_V), BF16),
        grid=(B,),
        in_specs=[
            pl.BlockSpec((1, H, Ls, QK_PAD), lambda b: (b, 0, 0, 0)),
            pl.BlockSpec((1, H, Ls, QK_PAD), lambda b: (b, 0, 0, 0)),
            pl.BlockSpec((1, 1, Ls, MLA_KV_RANK), lambda b: (layer, b, 0, 0)),
            pl.BlockSpec((1, P, MLA_KV_RANK), lambda b: (b, 0, 0)),
            pl.BlockSpec((1, P, MLA_ROPE), lambda b: (b, 0, 0)),
            full(wuk.shape), full(wuv.shape),
        ],
        out_specs=pl.BlockSpec((1, Ls, H * MLA_V), lambda b: (b, 0, 0)),
        compiler_params=_params("parallel"), name="attn_sample",
    )(q, kn, lat_buf, latp, kpep, wuk, wuv)


DIAG_BLOCK = SUBLANES
HGRN_UNROLL = 4


def _hgrn_kernel(h4_ref, lbl_ref, gn_ref, s0_ref, sbuf_in, o_ref, sout_ref, st_ref, pb_ref, ob_ref,
                 *, layer, c, nchunk, unroll):
    del sbuf_in
    t = pl.program_id(1)

    @pl.when(t == 0)
    def _():
        for h in range(HG_HEADS):
            st_ref[h] = s0_ref[0, h].T

    lg = lbl_ref[...]
    e = jnp.exp(lg - jnp.max(lg, axis=0, keepdims=True))
    sm = e / jnp.sum(e, axis=0, keepdims=True)
    lb_all = sm[0:1] * 0.0
    for i in range(1, layer + 1):
        lb_all = lb_all + sm[i:i + 1]
    gn = gn_ref[...]

    rt = lax.broadcasted_iota(jnp.int32, (c, c), 0)
    ct = lax.broadcasted_iota(jnp.int32, (c, c), 1)
    tril = jnp.where(ct <= rt, 1.0, 0.0).astype(BF16)
    nblk = c // DIAG_BLOCK
    qk_scale = HG_DK ** -0.5

    def chunk(ci, u):
        r0 = pl.multiple_of(ci * c, c)
        for h in range(HG_HEADS):
            sl = u * HG_HEADS + h

            def seg(k):
                return h4_ref[0, pl.ds(r0, c), pl.ds(k * HG_W + h * HG_DK, HG_DK)]
            hq, hf, hi, hg = seg(0), seg(1), seg(2), seg(3)
            lb = lb_all[:, h * HG_DK:(h + 1) * HG_DK]
            f = lb + (1.0 - lb) * _sigmoid(hf)
            logf = jnp.log(f)
            kk = 1.0 - f
            qq = hq * _sigmoid(hq) * qk_scale
            vb = hi.astype(BF16)

            p0 = logf.astype(BF16)
            r1 = logf - p0.astype(F32)
            p1 = r1.astype(BF16)
            p2 = (r1 - p1.astype(F32)).astype(BF16)
            b = _dot(tril, p0) + _dot(tril, p1) + _dot(tril, p2)
            bend = b[c - 1:c, :]

            st = st_ref[h]
            o = _dot_nt((qq * jnp.exp(b)).astype(BF16), st.astype(BF16))
            kd = (kk * jnp.exp(bend - b)).astype(BF16)
            st_ref[h] = st * jnp.exp(bend) + lax.dot_general(vb, kd, TN_DIMS, preferred_element_type=F32)

            a = jnp.zeros((c, c), F32)
            n = c // 2
            while n >= DIAG_BLOCK:
                qparts, kparts = [], []
                for j in range(c // (2 * n)):
                    lo, mid, hi_ = 2 * n * j, 2 * n * j + n, 2 * n * (j + 1)
                    ref = b[mid - 1:mid, :]
                    qparts += [jnp.zeros((n, HG_DK), F32), qq[mid:hi_] * jnp.exp(b[mid:hi_] - ref)]
                    kparts += [kk[lo:mid] * jnp.exp(ref - b[lo:mid]), jnp.zeros((n, HG_DK), F32)]
                an = _dot_nt(jnp.concatenate(qparts, axis=0).astype(BF16),
                             jnp.concatenate(kparts, axis=0).astype(BF16))
                a = a + jnp.where((rt // (2 * n)) == (ct // (2 * n)), an, 0.0)
                n //= 2
            o = o + _dot(a.astype(BF16), vb)

            pb_ref[sl, 0] = qq
            pb_ref[sl, 1] = kk
            pb_ref[sl, 2] = b
            pb_ref[sl, 3] = hi
            rows = [[pb_ref[sl, k, pl.ds(a, nblk, stride=DIAG_BLOCK), :] for a in range(DIAG_BLOCK)] for k in range(4)]
            for a in range(DIAG_BLOCK):
                oa = jnp.zeros((nblk, HG_DV), F32)
                for e_ in range(a + 1):
                    w = rows[0][a] * rows[1][e_]
                    if e_ < a:
                        w = w * jnp.exp(rows[2][a] - rows[2][e_])
                    oa = oa + jnp.sum(w, axis=1, keepdims=True) * rows[3][e_]
                ob_ref[sl, pl.ds(a, nblk, stride=DIAG_BLOCK), :] = oa
            o = o + ob_ref[sl]

            ob = _rms(o, gn) * (hg * _sigmoid(hg))
            o_ref[0, pl.ds(r0, c), pl.ds(h * HG_DV, HG_DV)] = ob.astype(BF16)

    def unrolled(cu, carry):
        for u in range(unroll):
            chunk(cu * unroll + u, u)
        return carry

    lax.fori_loop(0, nchunk // unroll, unrolled, 0)

    @pl.when(t == pl.num_programs(1) - 1)
    def _():
        for h in range(HG_HEADS):
            sout_ref[0, 0, h] = st_ref[h].T


def _hgrn(h4, lb_logits, gnorm, s0, s_buf, layer, tt):
    B, L, _ = h4.shape
    c = min(CHUNK, L)
    tt = min(tt, L)
    nchunk = tt // c
    unroll = HGRN_UNROLL if nchunk % HGRN_UNROLL == 0 else 1
    full = _resident
    return pl.pallas_call(
        functools.partial(_hgrn_kernel, layer=layer, c=c, nchunk=nchunk, unroll=unroll),
        out_shape=(jax.ShapeDtypeStruct((B, L, HG_HEADS * HG_DV), BF16),
                   jax.ShapeDtypeStruct(s_buf.shape, F32)),
        grid=(B, L // tt),
        in_specs=[
            pl.BlockSpec((1, tt, 4 * HG_W), lambda b, t: (b, t, 0)),
            full(lb_logits.shape), full(gnorm.shape),
            pl.BlockSpec((1, HG_HEADS, HG_DK, HG_DV), lambda b, t: (b, 0, 0, 0)),
            _aliased(),
        ],
        out_specs=(pl.BlockSpec((1, tt, HG_HEADS * HG_DV), lambda b, t: (b, t, 0)),
                   pl.BlockSpec((1, 1, HG_HEADS, HG_DK, HG_DV), lambda b, t: (layer, b, 0, 0, 0))),
        scratch_shapes=[pltpu.VMEM((HG_HEADS, HG_DV, HG_DK), F32),
                        pltpu.VMEM((unroll * HG_HEADS, 4, c, HG_DK), F32),
                        pltpu.VMEM((unroll * HG_HEADS, c, HG_DV), F32)],
        input_output_aliases={4: 1},
        compiler_params=_params("parallel", "arbitrary"), name="hgrn",
    )(h4, lb_logits, gnorm, s0, s_buf)


def _merge_kernel(x_ref, oa_ref, ob_ref, g_ref, wpa_ref, wpb_ref, wout_ref, y_ref, *, sb, tm):
    rows = sb * tm
    g = g_ref[...].reshape(rows, 2 * D_MODEL).astype(F32)
    oa = oa_ref[...].reshape(rows, MLA_HEADS * MLA_V)
    ob = ob_ref[...].reshape(rows, HG_HEADS * HG_DV)
    mix = g[:, :D_MODEL] * _dot(oa, wpa_ref[...]) + g[:, D_MODEL:] * _dot(ob, wpb_ref[...])
    y = x_ref[...].reshape(rows, D_MODEL) + _dot(mix.astype(BF16), wout_ref[...])
    y_ref[...] = y.reshape(sb, tm, D_MODEL)


def _merge(x, oa, ob, g, wpa, wpb, wout, sb, tm):
    B, L, _ = x.shape
    full = _resident
    row = lambda w: pl.BlockSpec((sb, tm, w), lambda b, t: (b, t, 0))
    return pl.pallas_call(
        functools.partial(_merge_kernel, sb=sb, tm=tm),
        out_shape=jax.ShapeDtypeStruct(x.shape, F32),
        grid=(B // sb, L // tm),
        in_specs=[row(D_MODEL), row(oa.shape[2]), row(ob.shape[2]), row(2 * D_MODEL),
                  full(wpa.shape), full(wpb.shape), full(wout.shape)],
        out_specs=row(D_MODEL),
        compiler_params=_params("parallel", "parallel"), name="merge",
    )(x, oa, ob, g, wpa, wpb, wout)


FFN_SPLIT = 2
FFN_COLS = FFN_DIM // FFN_SPLIT


def _gelu_tanh(x):
    return 0.5 * x * (1.0 + jnp.tanh(math.sqrt(2.0 / math.pi) * (x + 0.044715 * (x * x * x))))


def _ffn_kernel(x_ref, cs_ref, nf_ref, wup_ref, cw_ref, cb_ref, wdn_ref, nfin_ref, cbuf_in, y_ref, cnew_ref,
                carry_ref, *, final, sb, tm):
    del cbuf_in
    t = pl.program_id(1)
    rows = sb * tm

    @pl.when(t == 0)
    def _():
        carry_ref[:, 0:CONV_W - 1, :] = cs_ref[...]

    x = x_ref[...].reshape(rows, D_MODEL)
    hb = _rms(x, nf_ref[...]).astype(BF16)
    pos = lax.broadcasted_iota(jnp.int32, (sb, tm, FFN_COLS), 1).reshape(rows, FFN_COLS)

    def per_row(v):
        return jnp.broadcast_to(v, (sb, tm, FFN_COLS)).reshape(rows, FFN_COLS)

    acc = x
    for c in range(FFN_SPLIT):
        cs = slice(c * FFN_COLS, (c + 1) * FFN_COLS)
        a = _dot(hb, wup_ref[:, c * FFN_COLS:(c + 1) * FFN_COLS])
        val = _dot(hb, wup_ref[:, FFN_DIM + c * FFN_COLS:FFN_DIM + (c + 1) * FFN_COLS])
        prev2 = per_row(carry_ref[:, 0:1, cs])
        prev1 = per_row(carry_ref[:, 1:2, cs])
        a1 = jnp.where(pos == 0, prev1, pltpu.roll(a, 1, 0))
        a2 = jnp.where(pos == 0, prev2, jnp.where(pos == 1, prev1, pltpu.roll(a, 2, 0)))
        conv = cb_ref[:, cs] + cw_ref[0:1, cs] * a2 + cw_ref[1:2, cs] * a1 + cw_ref[2:3, cs] * a
        gated = (_gelu_tanh(conv) * val).astype(BF16)
        acc = acc + _dot(gated, wdn_ref[c * FFN_COLS:(c + 1) * FFN_COLS, :])
        carry_ref[:, 0:CONV_W - 1, cs] = a.reshape(sb, tm, FFN_COLS)[:, tm - (CONV_W - 1):tm, :]
    if final:
        acc = _rms(acc, nfin_ref[...])
    y_ref[...] = acc.reshape(sb, tm, D_MODEL)

    @pl.when(t == pl.num_programs(1) - 1)
    def _():
        cnew_ref[0] = carry_ref[:, 0:CONV_W - 1, :]


def _ffn(x, conv_state, nf, wup, cw, cb, wdn, nfin, c_buf, layer, final, sb, tm):
    B, L, _ = x.shape
    full = _resident
    return pl.pallas_call(
        functools.partial(_ffn_kernel, final=final, sb=sb, tm=tm),
        out_shape=(jax.ShapeDtypeStruct(x.shape, F32), jax.ShapeDtypeStruct(c_buf.shape, F32)),
        grid=(B // sb, L // tm),
        in_specs=[
            pl.BlockSpec((sb, tm, D_MODEL), lambda b, t: (b, t, 0)),
            pl.BlockSpec((sb, CONV_W - 1, FFN_DIM), lambda b, t: (b, 0, 0)),
            full(nf.shape), full(wup.shape), full(cw.shape), full(cb.shape), full(wdn.shape), full(nfin.shape),
            _aliased(),
        ],
        out_specs=(pl.BlockSpec((sb, tm, D_MODEL), lambda b, t: (b, t, 0)),
                   pl.BlockSpec((1, sb, CONV_W - 1, FFN_DIM), lambda b, t: (layer, b, 0, 0))),
        scratch_shapes=[pltpu.VMEM((sb, SUBLANES, FFN_DIM), F32)],
        input_output_aliases={8: 1},
        compiler_params=_params("parallel", "arbitrary"), name="ffn",
    )(x, conv_state, nf, wup, cw, cb, wdn, nfin, c_buf)


def _rope_tables(pos):
    half = MLA_ROPE // 2
    inv = ROPE_BASE ** (-jnp.arange(half, dtype=F32) / half)
    ang = pos.astype(F32)[:, None] * inv[None, :]
    cos, sin = jnp.cos(ang), jnp.sin(ang)
    return jnp.tile(cos, (1, 4)), jnp.concatenate([-sin, sin, -sin, sin], axis=1)


def _prep_layer(w_in_l, w_uq_l, w_ukv_l):
    kr = w_in_l[:, OFF_KR:OFF_KR + MLA_ROPE]
    win = jnp.concatenate([w_in_l[:, :OFF_KR], kr, kr, w_in_l[:, OFF_KR + MLA_ROPE:]], axis=1).astype(BF16)
    uq = w_uq_l.reshape(MLA_Q_RANK, MLA_HEADS, MLA_QK)
    wuq = jnp.concatenate([uq[:, :, :MLA_NOPE].reshape(MLA_Q_RANK, -1),
                           uq[:, :, MLA_NOPE:].reshape(MLA_Q_RANK, -1)], axis=1).astype(BF16)
    ukv = w_ukv_l.reshape(MLA_KV_RANK, MLA_HEADS, MLA_NOPE + MLA_V)
    wuk = ukv[:, :, :MLA_NOPE].reshape(MLA_KV_RANK, -1).astype(BF16)
    wuv = ukv[:, :, MLA_NOPE:].reshape(MLA_KV_RANK, -1).astype(BF16)
    return win, wuq, wuk, wuv


def _tiling(B, L):
    if L % ROW_TILE == 0:
        return 1, ROW_TILE
    sb = 1
    if L % SUBLANES == 0:
        for cand in range(1, B + 1):
            if B % cand == 0 and cand * L <= ROW_TILE:
                sb = cand
    return sb, L


def _state_buffers(depth, B, L):
    return (jnp.zeros((depth, B, L, MLA_KV_RANK), F32), jnp.zeros((depth, B, L, MLA_ROPE), F32),
            jnp.zeros((depth, B, HG_HEADS, HG_DK, HG_DV), F32), jnp.zeros((depth, B, CONV_W - 1, FFN_DIM), F32))


def kernel(x_prompt, x_sample, cache_mla_latent, cache_mla_krope, state_hgrn, state_ffn_conv, norm_mix, w_in, q_norm, w_uq, kv_norm, w_ukv, lb_logits, hgrn_norm, w_proj_a, w_proj_b, w_out, norm_ffn, w_up, conv_w, conv_b, w_down, norm_final):
    depth = w_in.shape[0]
    B, Lp, _ = x_prompt.shape
    Bs, Ls, _ = x_sample.shape
    past = cache_mla_latent.shape[2]

    sb_p, tm_p = _tiling(B, Lp)
    sb_s, tm_s = _tiling(Bs, Ls)
    cos_p, sin_p = _rope_tables(jnp.arange(Lp))
    cos_s, sin_s = _rope_tables(past + jnp.arange(Ls))
    cos_s, sin_s = jnp.tile(cos_s, (sb_s, 1)), jnp.tile(sin_s, (sb_s, 1))
    s0_p = jnp.zeros((B, HG_HEADS, HG_DK, HG_DV), F32)
    conv0_p = jnp.zeros((B, CONV_W - 1, FFN_DIM), F32)
    nfin = norm_final.reshape(1, D_MODEL)
    lbl = lb_logits.astype(F32)

    tq = ATTN_TQ if Lp % ATTN_TQ == 0 else Lp
    tk = ATTN_TK if tq % ATTN_TK == 0 else tq

    lat_p, kpe_p, hs_p, cv_p = _state_buffers(depth, B, Lp)
    lat_s, kpe_s, hs_s, cv_s = _state_buffers(depth, Bs, Ls)

    xp, xs = x_prompt, x_sample
    for l in range(depth):
        win, wuq, wuk, wuv = _prep_layer(w_in[l], w_uq[l], w_ukv[l])
        wuvt = wuv.T
        nmix = norm_mix[l].reshape(1, -1)
        qn = q_norm[l].reshape(1, -1)
        kvn = kv_norm[l].reshape(1, -1)
        gn = hgrn_norm[l].reshape(1, -1)
        nf = norm_ffn[l].reshape(1, -1)
        wpa, wpb, wo = w_proj_a[l].astype(BF16), w_proj_b[l].astype(BF16), w_out[l].astype(BF16)
        wup, wdn = w_up[l].astype(BF16), w_down[l].astype(BF16)
        cw, cb = conv_w[l], conv_b[l].reshape(1, -1)
        final = l == depth - 1

        q, k, vt, lat_p, kpe_p, h4, g = _inproj(xp, cos_p, sin_p, nmix, win, qn, wuq, kvn, wuk, wuvt,
                                                lat_p, kpe_p, l, sb_p, tm_p, True)
        oa = _attn_prompt(q, k, vt, tq, tk)
        ob, hs_p = _hgrn(h4, lbl, gn, s0_p, hs_p, l, 512)
        x1 = _merge(xp, oa, ob, g, wpa, wpb, wo, sb_p, tm_p)
        xp, cv_p = _ffn(x1, conv0_p, nf, wup, cw, cb, wdn, nfin, cv_p, l, final, sb_p, tm_p)

        q, k, lat_s, kpe_s, h4, g = _inproj(xs, cos_s, sin_s, nmix, win, qn, wuq, kvn, wuk, wuvt,
                                            lat_s, kpe_s, l, sb_s, tm_s, False)
        oa = _attn_sample(q, k, lat_s, l, cache_mla_latent[l], cache_mla_krope[l], wuk, wuv)
        ob, hs_s = _hgrn(h4, lbl, gn, state_hgrn[l], hs_s, l, 512)
        x1 = _merge(xs, oa, ob, g, wpa, wpb, wo, sb_s, tm_s)
        xs, cv_s = _ffn(x1, state_ffn_conv[l], nf, wup, cw, cb, wdn, nfin, cv_s, l, final, sb_s, tm_s)

    return (xp, xs, lat_p, kpe_p, hs_p, cv_p, lat_s, kpe_s, hs_s, cv_s)
```

```python
import functools
import math

import numpy as np
import jax
import jax.numpy as jnp
from jax import lax
from jax.experimental import pallas as pl
from jax.experimental.pallas import tpu as pltpu

D_MODEL = 1024
CHUNK = 64
EPS = 1e-6
MLA_HEADS = 8
MLA_Q_RANK = 384
MLA_KV_RANK = 256
MLA_NOPE = 128
MLA_ROPE = 64
MLA_V = 128
ROPE_BASE = 10000.0
HG_HEADS = 4
HG_DK = 128
HG_DV = 128
FFN_DIM = 2816
CONV_W = 3

MLA_QK = MLA_NOPE + MLA_ROPE
HG_W = HG_HEADS * HG_DK
QK_PAD = 2 * MLA_NOPE
VT_ROWS = MLA_V + 16
LANES = 128
SUBLANES = 8
VMEM_LIMIT = 56 * 1024 * 1024
ROW_TILE = 512

OFF_CQ = 0
OFF_CKV = OFF_CQ + MLA_Q_RANK
OFF_KR = OFF_CKV + MLA_KV_RANK
OFF_HG = OFF_KR + 2 * MLA_ROPE
OFF_GATE = OFF_HG + 4 * HG_W
IN_COLS_R = OFF_GATE + 2 * D_MODEL

BF16 = jnp.bfloat16
F32 = jnp.float32
NT_DIMS = (((1,), (1,)), ((), ()))
TN_DIMS = (((0,), (0,)), ((), ()))


def _dot(a, b):
    return jnp.dot(a, b, preferred_element_type=F32)


def _dot_nt(a, b):
    return lax.dot_general(a, b, NT_DIMS, preferred_element_type=F32)


def _rms(x, g):
    return x * lax.rsqrt(jnp.mean(x * x, axis=-1, keepdims=True) + EPS) * g


def _sigmoid(x):
    return 0.5 * (jnp.tanh(0.5 * x) + 1.0)


def _resident(shape):
    return pl.BlockSpec(shape, lambda *_: (0,) * len(shape), pipeline_mode=pl.Buffered(1))


def _aliased():
    return pl.BlockSpec(memory_space=pl.ANY)


def _params(*sem):
    return pltpu.CompilerParams(dimension_semantics=sem, vmem_limit_bytes=VMEM_LIMIT)


def _inproj_kernel(x_ref, cos_ref, sin_ref, nmix_ref, win_ref, qn_ref, wuq_ref, kvn_ref, wuk_ref, wuvt_ref, lbl_ref,
                   lat_in, kpe_in, *out_refs, qscale, layer, sb, tm, with_vt):
    del lat_in, kpe_in
    if with_vt:
        q_ref, k_ref, vt_ref, lat_ref, kpe_ref, h4_ref, g_ref = out_refs
    else:
        q_ref, k_ref, lat_ref, kpe_ref, h4_ref, g_ref = out_refs
    rows = sb * tm
    x = x_ref[...].reshape(rows, D_MODEL)
    hb = _rms(x, nmix_ref[...]).astype(BF16)
    cosf = cos_ref[...]
    sinf = sin_ref[...]
    lane = lax.broadcasted_iota(jnp.int32, cosf.shape, 1)
    first_half = (lane % MLA_ROPE) < (MLA_ROPE // 2)
    low_head = lane < MLA_ROPE

    def rot(v):
        sw = jnp.where(first_half, pltpu.roll(v, LANES - MLA_ROPE // 2, 1), pltpu.roll(v, MLA_ROPE // 2, 1))
        return v * cosf + sw * sinf

    cq = _dot(hb, win_ref[:, OFF_CQ:OFF_CKV])
    cqn = _rms(cq, qn_ref[...]).astype(BF16)
    q = _dot(cqn, wuq_ref[...]) * qscale

    ckv = _dot(hb, win_ref[:, OFF_CKV:OFF_KR])
    lat = _rms(ckv, kvn_ref[...])
    lat_ref[0] = lat.reshape(sb, tm, MLA_KV_RANK)
    latb = lat.astype(BF16)
    kn = _dot(latb, wuk_ref[...])
    if with_vt:
        vt = _dot_nt(wuvt_ref[...], latb).astype(BF16)
        ones = jnp.ones((VT_ROWS - MLA_V, rows), BF16)
        for h in range(MLA_HEADS):
            vt_ref[0, h * VT_ROWS:h * VT_ROWS + MLA_V, :] = vt[h * MLA_V:(h + 1) * MLA_V]
            vt_ref[0, h * VT_ROWS + MLA_V:(h + 1) * VT_ROWS, :] = ones

    kpe = rot(_dot(hb, win_ref[:, OFF_KR:OFF_HG]))
    kpe_ref[0] = kpe[:, :MLA_ROPE].reshape(sb, tm, MLA_ROPE)
    kpeb = kpe.astype(BF16)

    nope_w = MLA_HEADS * MLA_NOPE
    for g in range(MLA_HEADS // 2):
        r = rot(q[:, nope_w + g * LANES: nope_w + (g + 1) * LANES])
        for hh in range(2):
            h = 2 * g + hh
            rh = jnp.where(low_head if hh == 0 else jnp.logical_not(low_head), r, 0.0)
            qh = jnp.concatenate([q[:, h * MLA_NOPE:(h + 1) * MLA_NOPE], rh], axis=1).astype(BF16)
            kh = jnp.concatenate([kn[:, h * MLA_NOPE:(h + 1) * MLA_NOPE].astype(BF16), kpeb], axis=1)
            q_ref[:, h] = qh.reshape(sb, tm, QK_PAD)
            k_ref[:, h] = kh.reshape(sb, tm, QK_PAD)

    lg = lbl_ref[...]
    e = jnp.exp(lg - jnp.max(lg, axis=0, keepdims=True))
    sm = e / jnp.sum(e, axis=0, keepdims=True)
    lb = sm[0:1] * 0.0
    for i in range(1, layer + 1):
        lb = lb + sm[i:i + 1]
    h4 = _dot(hb, win_ref[:, OFF_HG:OFF_GATE])
    hq, hf, hi, hg = (h4[:, k * HG_W:(k + 1) * HG_W] for k in range(4))
    h4 = jnp.concatenate([hq * _sigmoid(hq) * HG_DK ** -0.5, lb + (1.0 - lb) * _sigmoid(hf), hi, hg * _sigmoid(hg)],
                         axis=1)
    h4_ref[...] = h4.reshape(sb, tm, 4 * HG_W)
    gates = _sigmoid(_dot(hb, win_ref[:, OFF_GATE:IN_COLS_R]))
    g_ref[...] = gates.astype(BF16).reshape(sb, tm, 2 * D_MODEL)


def _inproj(x, cosf, sinf, nmix, win, qn, wuq, kvn, wuk, wuvt, lbl, lat_buf, kpe_buf, layer, sb, tm, with_vt):
    B, L, _ = x.shape
    H = MLA_HEADS
    qscale = (1.0 / math.sqrt(MLA_QK)) * math.log2(math.e)
    full = _resident
    seq = lambda w: pl.BlockSpec((sb, tm, w), lambda b, t: (b, t, 0))
    stacked = lambda w: pl.BlockSpec((1, sb, tm, w), lambda b, t: (layer, b, t, 0))
    qk_shape = jax.ShapeDtypeStruct((B, H, L, QK_PAD), BF16)
    qk_spec = pl.BlockSpec((sb, H, tm, QK_PAD), lambda b, t: (b, 0, t, 0))
    out_shape = [qk_shape, qk_shape]
    out_specs = [qk_spec, qk_spec]
    if with_vt:
        assert sb == 1
        out_shape.append(jax.ShapeDtypeStruct((B, H * VT_ROWS, L), BF16))
        out_specs.append(pl.BlockSpec((1, H * VT_ROWS, tm), lambda b, t: (b, 0, t)))
    lat_idx = len(out_shape)
    out_shape += [jax.ShapeDtypeStruct(lat_buf.shape, F32), jax.ShapeDtypeStruct(kpe_buf.shape, F32),
                  jax.ShapeDtypeStruct((B, L, 4 * HG_W), F32), jax.ShapeDtypeStruct((B, L, 2 * D_MODEL), BF16)]
    out_specs += [stacked(MLA_KV_RANK), stacked(MLA_ROPE), seq(4 * HG_W), seq(2 * D_MODEL)]
    in_specs = [
        seq(D_MODEL),
        pl.BlockSpec((sb * tm, LANES), lambda b, t: (t, 0)),
        pl.BlockSpec((sb * tm, LANES), lambda b, t: (t, 0)),
        full(nmix.shape), full(win.shape), full(qn.shape), full(wuq.shape), full(kvn.shape),
        full(wuk.shape), full(wuvt.shape), full(lbl.shape),
        _aliased(), _aliased(),
    ]
    return pl.pallas_call(
        functools.partial(_inproj_kernel, qscale=qscale, layer=layer, sb=sb, tm=tm, with_vt=with_vt),
        out_shape=tuple(out_shape), grid=(B // sb, L // tm), in_specs=in_specs, out_specs=tuple(out_specs),
        input_output_aliases={11: lat_idx, 12: lat_idx + 1},
        compiler_params=_params("parallel", "arbitrary"), name="inproj",
    )(x, cosf, sinf, nmix, win, qn, wuq, kvn, wuk, wuvt, lbl, lat_buf, kpe_buf)


ATTN_HEAD_GROUP = 2
ATTN_TQ = 2048
ATTN_TK = 512


def _attn_kernel(q_ref, k_ref, vt_ref, o_ref, acc_ref, *, tq, tk):
    i = pl.program_id(2)
    G = ATTN_HEAD_GROUP
    r = tq // tk

    def step(j, ms, q0):
        diagonal = q0 is not None
        q0 = q0 or 0
        off = pl.multiple_of(j * tk, tk)
        out = []
        for g in range(G):
            m = ms[g][:, q0:]
            kb = k_ref[0, g, pl.ds(off, tk), :]
            s = _dot_nt(kb, q_ref[0, g, q0:, :])
            if diagonal:
                kc = lax.broadcasted_iota(jnp.int32, (tk, tk), 0) // CHUNK
                qc = lax.broadcasted_iota(jnp.int32, (tk, tk), 1) // CHUNK
                sm = jnp.where(kc <= qc, s[:, :tk], -jnp.inf)
                s = sm if s.shape[1] == tk else jnp.concatenate([sm, s[:, tk:]], axis=1)
            m_new = jnp.maximum(m, jnp.max(s, axis=0, keepdims=True))
            alpha = jnp.exp2(m - m_new)
            p = jnp.exp2(s - m_new).astype(BF16)
            vb = vt_ref[0, g * VT_ROWS:(g + 1) * VT_ROWS, pl.ds(off, tk)]
            acc_ref[g, :, q0:] = alpha * acc_ref[g, :, q0:] + _dot(vb, p)
            out.append(m_new if q0 == 0 else jnp.concatenate([ms[g][:, :q0], m_new], axis=1))
        return tuple(out)

    acc_ref[...] = jnp.zeros_like(acc_ref)
    ms = tuple(jnp.full((1, tq), -jnp.inf, F32) for _ in range(G))
    ms = lax.fori_loop(0, i * r, lambda j, c: step(j, c, None), ms)
    for d in range(r):
        ms = step(i * r + d, ms, d * tk)
    for g in range(G):
        acc = acc_ref[g]
        o_ref[0, :, g * MLA_V:(g + 1) * MLA_V] = (acc[:MLA_V] / acc[MLA_V:MLA_V + 1]).T.astype(BF16)


def _attn_prompt(q, k, vt, tq, tk):
    B, H, L, _ = q.shape
    G = ATTN_HEAD_GROUP
    return pl.pallas_call(
        functools.partial(_attn_kernel, tq=tq, tk=tk),
        out_shape=jax.ShapeDtypeStruct((B, L, H * MLA_V), BF16),
        grid=(B, H // G, L // tq),
        in_specs=[
            pl.BlockSpec((1, G, tq, QK_PAD), lambda b, h, i: (b, h, i, 0)),
            pl.BlockSpec((1, G, L, QK_PAD), lambda b, h, i: (b, h, 0, 0)),
            pl.BlockSpec((1, G * VT_ROWS, L), lambda b, h, i: (b, h, 0)),
        ],
        out_specs=pl.BlockSpec((1, tq, G * MLA_V), lambda b, h, i: (b, i, h)),
        scratch_shapes=[pltpu.VMEM((G, VT_ROWS, tq), F32)],
        compiler_params=_params("parallel", "parallel", "arbitrary"), name="attn_prompt",
    )(q, k, vt)


def _attn_sample_kernel(q_ref, kn_ref, latn_ref, latp_ref, kpep_ref, wuk_ref, wuv_ref, o_ref):
    latp = latp_ref[0].astype(BF16)
    latn = latn_ref[0, 0].astype(BF16)
    knp = _dot(latp, wuk_ref[...]).astype(BF16)
    vp = _dot(latp, wuv_ref[...]).astype(BF16)
    vn = _dot(latn, wuv_ref[...]).astype(BF16)
    kp = kpep_ref[0].astype(BF16)
    kpd = jnp.concatenate([kp, kp], axis=1)
    for h in range(MLA_HEADS):
        sl = slice(h * MLA_NOPE, (h + 1) * MLA_NOPE)
        qh = q_ref[0, h]
        khp = jnp.concatenate([knp[:, sl], kpd], axis=1)
        s1 = _dot_nt(qh, khp)
        s2 = _dot_nt(qh, kn_ref[0, h])
        m = jnp.maximum(jnp.max(s1, axis=1, keepdims=True), jnp.max(s2, axis=1, keepdims=True))
        p1 = jnp.exp2(s1 - m)
        p2 = jnp.exp2(s2 - m)
        l = jnp.sum(p1, axis=1, keepdims=True) + jnp.sum(p2, axis=1, keepdims=True)
        o = _dot(p1.astype(BF16), vp[:, sl]) + _dot(p2.astype(BF16), vn[:, sl])
        o_ref[0, :, sl] = (o / l).astype(BF16)


def _attn_sample(q, kn, lat_buf, layer, latp, kpep, wuk, wuv):
    B, H, Ls, _ = q.shape
    P = latp.shape[1]
    full = _resident
    return pl.pallas_call(
        _attn_sample_kernel,
        out_shape=jax.ShapeDtypeStruct((B, Ls, H * MLA_V), BF16),
        grid=(B,),
        in_specs=[
            pl.BlockSpec((1, H, Ls, QK_PAD), lambda b: (b, 0, 0, 0)),
            pl.BlockSpec((1, H, Ls, QK_PAD), lambda b: (b, 0, 0, 0)),
            pl.BlockSpec((1, 1, Ls, MLA_KV_RANK), lambda b: (layer, b, 0, 0)),
            pl.BlockSpec((1, P, MLA_KV_RANK), lambda b: (b, 0, 0)),
            pl.BlockSpec((1, P, MLA_ROPE), lambda b: (b, 0, 0)),
            full(wuk.shape), full(wuv.shape),
        ],
        out_specs=pl.BlockSpec((1, Ls, H * MLA_V), lambda b: (b, 0, 0)),
        compiler_params=_params("parallel"), name="attn_sample",
    )(q, kn, lat_buf, latp, kpep, wuk, wuv)


DIAG_BLOCK = SUBLANES
HGRN_UNROLL = 4


def _hgrn_kernel(h4_ref, gn_ref, s0_ref, sbuf_in, o_ref, sout_ref, st_ref, pb_ref, ob_ref, lvl_ref,
                 *, c, nchunk, unroll):
    del sbuf_in
    t = pl.program_id(1)

    @pl.when(t == 0)
    def _():
        for h in range(HG_HEADS):
            st_ref[h] = s0_ref[0, h].T

    gn = gn_ref[...]

    rt = lax.broadcasted_iota(jnp.int32, (c, c), 0)
    ct = lax.broadcasted_iota(jnp.int32, (c, c), 1)
    tril = jnp.where(ct <= rt, 1.0, 0.0).astype(BF16)
    tril3 = jnp.concatenate([tril, tril, tril], axis=1)
    rt4 = lax.broadcasted_iota(jnp.int32, (HG_HEADS * c, HG_HEADS * c), 0)
    ct4 = lax.broadcasted_iota(jnp.int32, (HG_HEADS * c, HG_HEADS * c), 1)
    lvl = jnp.zeros((HG_HEADS * c, HG_HEADS * c), jnp.int32)
    n, level = c // 2, 0
    while n >= DIAG_BLOCK:
        level += 1
        hit = ((rt4 // (2 * n)) == (ct4 // (2 * n))) & ((rt4 % (2 * n)) >= n) & ((ct4 % (2 * n)) < n)
        lvl = jnp.where(hit, level, lvl)
        n //= 2
    lvl_ref[...] = lvl
    nblk = c // DIAG_BLOCK

    def chunk(ci, u):
        r0 = pl.multiple_of(ci * c, c)

        def seg(k):
            return h4_ref[0, pl.ds(r0, c), pl.ds(k * HG_W, HG_W)]

        def stack(w):
            return jnp.concatenate([w[:, h * HG_DK:(h + 1) * HG_DK] for h in range(HG_HEADS)], axis=0)

        qq_w, f_w, hi_w, gs_w = seg(0), seg(1), seg(2), seg(3)
        logf = jnp.log(f_w)
        kk_w = 1.0 - f_w
        vb_w = hi_w.astype(BF16)

        p0 = logf.astype(BF16)
        r1 = logf - p0.astype(F32)
        p1 = r1.astype(BF16)
        p2 = (r1 - p1.astype(F32)).astype(BF16)
        b_w = _dot(tril3, jnp.concatenate([p0, p1, p2], axis=0))
        bend_w = b_w[c - 1:c, :]
        ebend_w = jnp.exp(bend_w)

        inter = _dot_nt(stack((qq_w * jnp.exp(b_w)).astype(BF16)),
                        st_ref[...].reshape(HG_HEADS * HG_DV, HG_DK).astype(BF16))
        kd_w = (kk_w * jnp.exp(bend_w - b_w)).astype(BF16)
        upd = lax.dot_general(vb_w, kd_w, TN_DIMS, preferred_element_type=F32)
        for h in range(HG_HEADS):
            hs = slice(h * HG_DK, (h + 1) * HG_DK)
            st_ref[h] = st_ref[h] * ebend_w[:, hs] + upd[hs, hs]

        a_all = jnp.zeros((HG_HEADS * c, HG_HEADS * c), F32)
        n = c // 2
        level = 0
        while n >= DIAG_BLOCK:
            qparts, kparts = [], []
            for j in range(c // (2 * n)):
                lo, mid, hi_ = 2 * n * j, 2 * n * j + n, 2 * n * (j + 1)
                ref = b_w[mid - 1:mid, :]
                qparts += [jnp.zeros((n, HG_W), F32), qq_w[mid:hi_] * jnp.exp(b_w[mid:hi_] - ref)]
                kparts += [kk_w[lo:mid] * jnp.exp(ref - b_w[lo:mid]), jnp.zeros((n, HG_W), F32)]
            an = _dot_nt(stack(jnp.concatenate(qparts, axis=0)).astype(BF16),
                         stack(jnp.concatenate(kparts, axis=0)).astype(BF16))
            level += 1
            a_all = jnp.where(lvl_ref[...] == level, an, a_all)
            n //= 2
        intra = _dot(a_all.astype(BF16), stack(vb_w))

        for h in range(HG_HEADS):
            sl = u * HG_HEADS + h
            hs = slice(h * HG_DK, (h + 1) * HG_DK)
            qq, kk, b, hi = qq_w[:, hs], kk_w[:, hs], b_w[:, hs], hi_w[:, hs]
            o = inter[h * c:(h + 1) * c, hs] + intra[h * c:(h + 1) * c]

            pb_ref[sl, 0] = qq
            pb_ref[sl, 1] = kk
            pb_ref[sl, 2] = b
            pb_ref[sl, 3] = hi
            rows = [[pb_ref[sl, k, pl.ds(a, nblk, stride=DIAG_BLOCK), :] for a in range(DIAG_BLOCK)] for k in range(4)]
            for a in range(DIAG_BLOCK):
                oa = jnp.zeros((nblk, HG_DV), F32)
                for e_ in range(a + 1):
                    w = rows[0][a] * rows[1][e_]
                    if e_ < a:
                        w = w * jnp.exp(rows[2][a] - rows[2][e_])
                    oa = oa + jnp.sum(w, axis=1, keepdims=True) * rows[3][e_]
                ob_ref[sl, pl.ds(a, nblk, stride=DIAG_BLOCK), :] = oa
            o = o + ob_ref[sl]

            ob = _rms(o, gn) * gs_w[:, hs]
            o_ref[0, pl.ds(r0, c), pl.ds(h * HG_DV, HG_DV)] = ob.astype(BF16)

    def unrolled(cu, carry):
        for u in range(unroll):
            chunk(cu * unroll + u, u)
        return carry

    lax.fori_loop(0, nchunk // unroll, unrolled, 0)

    @pl.when(t == pl.num_programs(1) - 1)
    def _():
        for h in range(HG_HEADS):
            sout_ref[0, 0, h] = st_ref[h].T


def _hgrn(h4, gnorm, s0, s_buf, layer, tt):
    B, L, _ = h4.shape
    c = min(CHUNK, L)
    tt = min(tt, L)
    nchunk = tt // c
    unroll = HGRN_UNROLL if nchunk % HGRN_UNROLL == 0 else 1
    full = _resident
    return pl.pallas_call(
        functools.partial(_hgrn_kernel, c=c, nchunk=nchunk, unroll=unroll),
        out_shape=(jax.ShapeDtypeStruct((B, L, HG_HEADS * HG_DV), BF16),
                   jax.ShapeDtypeStruct(s_buf.shape, F32)),
        grid=(B, L // tt),
        in_specs=[
            pl.BlockSpec((1, tt, 4 * HG_W), lambda b, t: (b, t, 0)),
            full(gnorm.shape),
            pl.BlockSpec((1, HG_HEADS, HG_DK, HG_DV), lambda b, t: (b, 0, 0, 0)),
            _aliased(),
        ],
        out_specs=(pl.BlockSpec((1, tt, HG_HEADS * HG_DV), lambda b, t: (b, t, 0)),
                   pl.BlockSpec((1, 1, HG_HEADS, HG_DK, HG_DV), lambda b, t: (layer, b, 0, 0, 0))),
        scratch_shapes=[pltpu.VMEM((HG_HEADS, HG_DV, HG_DK), F32),
                        pltpu.VMEM((unroll * HG_HEADS, 4, c, HG_DK), F32),
                        pltpu.VMEM((unroll * HG_HEADS, c, HG_DV), F32),
                        pltpu.VMEM((HG_HEADS * c, HG_HEADS * c), jnp.int32)],
        input_output_aliases={3: 1},
        compiler_params=_params("parallel", "arbitrary"), name="hgrn",
    )(h4, gnorm, s0, s_buf)


def _merge_kernel(x_ref, oa_ref, ob_ref, g_ref, wpa_ref, wpb_ref, wout_ref, y_ref, *, sb, tm):
    rows = sb * tm
    g = g_ref[...].reshape(rows, 2 * D_MODEL).astype(F32)
    oa = oa_ref[...].reshape(rows, MLA_HEADS * MLA_V)
    ob = ob_ref[...].reshape(rows, HG_HEADS * HG_DV)
    mix = g[:, :D_MODEL] * _dot(oa, wpa_ref[...]) + g[:, D_MODEL:] * _dot(ob, wpb_ref[...])
    y = x_ref[...].reshape(rows, D_MODEL) + _dot(mix.astype(BF16), wout_ref[...])
    y_ref[...] = y.reshape(sb, tm, D_MODEL)


def _merge(x, oa, ob, g, wpa, wpb, wout, sb, tm):
    B, L, _ = x.shape
    full = _resident
    row = lambda w: pl.BlockSpec((sb, tm, w), lambda b, t: (b, t, 0))
    return pl.pallas_call(
        functools.partial(_merge_kernel, sb=sb, tm=tm),
        out_shape=jax.ShapeDtypeStruct(x.shape, F32),
        grid=(B // sb, L // tm),
        in_specs=[row(D_MODEL), row(oa.shape[2]), row(ob.shape[2]), row(2 * D_MODEL),
                  full(wpa.shape), full(wpb.shape), full(wout.shape)],
        out_specs=row(D_MODEL),
        compiler_params=_params("parallel", "parallel"), name="merge",
    )(x, oa, ob, g, wpa, wpb, wout)


FFN_SPLIT = 2
FFN_COLS = FFN_DIM // FFN_SPLIT


def _gelu_tanh(x):
    return 0.5 * x * (1.0 + jnp.tanh(math.sqrt(2.0 / math.pi) * (x + 0.044715 * (x * x * x))))


def _ffn_kernel(x_ref, cs_ref, nf_ref, wup_ref, cw_ref, cb_ref, wdn_ref, nfin_ref, cbuf_in, y_ref, cnew_ref,
                carry_ref, *, final, sb, tm):
    del cbuf_in
    t = pl.program_id(1)
    rows = sb * tm

    @pl.when(t == 0)
    def _():
        carry_ref[:, 0:CONV_W - 1, :] = cs_ref[...]

    x = x_ref[...].reshape(rows, D_MODEL)
    hb = _rms(x, nf_ref[...]).astype(BF16)
    pos = lax.broadcasted_iota(jnp.int32, (sb, tm, FFN_COLS), 1).reshape(rows, FFN_COLS)

    def per_row(v):
        return jnp.broadcast_to(v, (sb, tm, FFN_COLS)).reshape(rows, FFN_COLS)

    acc = x
    for c in range(FFN_SPLIT):
        cs = slice(c * FFN_COLS, (c + 1) * FFN_COLS)
        a = _dot(hb, wup_ref[:, c * FFN_COLS:(c + 1) * FFN_COLS])
        val = _dot(hb, wup_ref[:, FFN_DIM + c * FFN_COLS:FFN_DIM + (c + 1) * FFN_COLS])
        prev2 = per_row(carry_ref[:, 0:1, cs])
        prev1 = per_row(carry_ref[:, 1:2, cs])
        a1 = jnp.where(pos == 0, prev1, pltpu.roll(a, 1, 0))
        a2 = jnp.where(pos == 0, prev2, jnp.where(pos == 1, prev1, pltpu.roll(a, 2, 0)))
        conv = cb_ref[:, cs] + cw_ref[0:1, cs] * a2 + cw_ref[1:2, cs] * a1 + cw_ref[2:3, cs] * a
        gated = (_gelu_tanh(conv) * val).astype(BF16)
        acc = acc + _dot(gated, wdn_ref[c * FFN_COLS:(c + 1) * FFN_COLS, :])
        carry_ref[:, 0:CONV_W - 1, cs] = a.reshape(sb, tm, FFN_COLS)[:, tm - (CONV_W - 1):tm, :]
    if final:
        acc = _rms(acc, nfin_ref[...])
    y_ref[...] = acc.reshape(sb, tm, D_MODEL)

    @pl.when(t == pl.num_programs(1) - 1)
    def _():
        cnew_ref[0] = carry_ref[:, 0:CONV_W - 1, :]


def _ffn(x, conv_state, nf, wup, cw, cb, wdn, nfin, c_buf, layer, final, sb, tm):
    B, L, _ = x.shape
    full = _resident
    return pl.pallas_call(
        functools.partial(_ffn_kernel, final=final, sb=sb, tm=tm),
        out_shape=(jax.ShapeDtypeStruct(x.shape, F32), jax.ShapeDtypeStruct(c_buf.shape, F32)),
        grid=(B // sb, L // tm),
        in_specs=[
            pl.BlockSpec((sb, tm, D_MODEL), lambda b, t: (b, t, 0)),
            pl.BlockSpec((sb, CONV_W - 1, FFN_DIM), lambda b, t: (b, 0, 0)),
            full(nf.shape), full(wup.shape), full(cw.shape), full(cb.shape), full(wdn.shape), full(nfin.shape),
            _aliased(),
        ],
        out_specs=(pl.BlockSpec((sb, tm, D_MODEL), lambda b, t: (b, t, 0)),
                   pl.BlockSpec((1, sb, CONV_W - 1, FFN_DIM), lambda b, t: (layer, b, 0, 0))),
        scratch_shapes=[pltpu.VMEM((sb, SUBLANES, FFN_DIM), F32)],
        input_output_aliases={8: 1},
        compiler_params=_params("parallel", "arbitrary"), name="ffn",
    )(x, conv_state, nf, wup, cw, cb, wdn, nfin, c_buf)


def _rope_tables(pos):
    half = MLA_ROPE // 2
    inv = ROPE_BASE ** (-jnp.arange(half, dtype=F32) / half)
    ang = pos.astype(F32)[:, None] * inv[None, :]
    cos, sin = jnp.cos(ang), jnp.sin(ang)
    return jnp.tile(cos, (1, 4)), jnp.concatenate([-sin, sin, -sin, sin], axis=1)


def _prep_layer(w_in_l, w_uq_l, w_ukv_l):
    kr = w_in_l[:, OFF_KR:OFF_KR + MLA_ROPE]
    win = jnp.concatenate([w_in_l[:, :OFF_KR], kr, kr, w_in_l[:, OFF_KR + MLA_ROPE:]], axis=1).astype(BF16)
    uq = w_uq_l.reshape(MLA_Q_RANK, MLA_HEADS, MLA_QK)
    wuq = jnp.concatenate([uq[:, :, :MLA_NOPE].reshape(MLA_Q_RANK, -1),
                           uq[:, :, MLA_NOPE:].reshape(MLA_Q_RANK, -1)], axis=1).astype(BF16)
    ukv = w_ukv_l.reshape(MLA_KV_RANK, MLA_HEADS, MLA_NOPE + MLA_V)
    wuk = ukv[:, :, :MLA_NOPE].reshape(MLA_KV_RANK, -1).astype(BF16)
    wuv = ukv[:, :, MLA_NOPE:].reshape(MLA_KV_RANK, -1).astype(BF16)
    return win, wuq, wuk, wuv


def _tiling(B, L):
    if L % ROW_TILE == 0:
        return 1, ROW_TILE
    sb = 1
    if L % SUBLANES == 0:
        for cand in range(1, B + 1):
            if B % cand == 0 and cand * L <= ROW_TILE:
                sb = cand
    return sb, L


def _state_buffers(depth, B, L):
    return (jnp.zeros((depth, B, L, MLA_KV_RANK), F32), jnp.zeros((depth, B, L, MLA_ROPE), F32),
            jnp.zeros((depth, B, HG_HEADS, HG_DK, HG_DV), F32), jnp.zeros((depth, B, CONV_W - 1, FFN_DIM), F32))


def kernel(x_prompt, x_sample, cache_mla_latent, cache_mla_krope, state_hgrn, state_ffn_conv, norm_mix, w_in, q_norm, w_uq, kv_norm, w_ukv, lb_logits, hgrn_norm, w_proj_a, w_proj_b, w_out, norm_ffn, w_up, conv_w, conv_b, w_down, norm_final):
    depth = w_in.shape[0]
    B, Lp, _ = x_prompt.shape
    Bs, Ls, _ = x_sample.shape
    past = cache_mla_latent.shape[2]

    sb_p, tm_p = _tiling(B, Lp)
    sb_s, tm_s = _tiling(Bs, Ls)
    cos_p, sin_p = _rope_tables(jnp.arange(Lp))
    cos_s, sin_s = _rope_tables(past + jnp.arange(Ls))
    cos_s, sin_s = jnp.tile(cos_s, (sb_s, 1)), jnp.tile(sin_s, (sb_s, 1))
    s0_p = jnp.zeros((B, HG_HEADS, HG_DK, HG_DV), F32)
    conv0_p = jnp.zeros((B, CONV_W - 1, FFN_DIM), F32)
    nfin = norm_final.reshape(1, D_MODEL)
    lbl = lb_logits.astype(F32)

    tq = ATTN_TQ if Lp % ATTN_TQ == 0 else Lp
    tk = ATTN_TK if tq % ATTN_TK == 0 else tq

    lat_p, kpe_p, hs_p, cv_p = _state_buffers(depth, B, Lp)
    lat_s, kpe_s, hs_s, cv_s = _state_buffers(depth, Bs, Ls)

    xp, xs = x_prompt, x_sample
    for l in range(depth):
        win, wuq, wuk, wuv = _prep_layer(w_in[l], w_uq[l], w_ukv[l])
        wuvt = wuv.T
        nmix = norm_mix[l].reshape(1, -1)
        qn = q_norm[l].reshape(1, -1)
        kvn = kv_norm[l].reshape(1, -1)
        gn = hgrn_norm[l].reshape(1, -1)
        nf = norm_ffn[l].reshape(1, -1)
        wpa, wpb, wo = w_proj_a[l].astype(BF16), w_proj_b[l].astype(BF16), w_out[l].astype(BF16)
        wup, wdn = w_up[l].astype(BF16), w_down[l].astype(BF16)
        cw, cb = conv_w[l], conv_b[l].reshape(1, -1)
        final = l == depth - 1

        q, k, vt, lat_p, kpe_p, h4, g = _inproj(xp, cos_p, sin_p, nmix, win, qn, wuq, kvn, wuk, wuvt, lbl,
                                                lat_p, kpe_p, l, sb_p, tm_p, True)
        oa = _attn_prompt(q, k, vt, tq, tk)
        ob, hs_p = _hgrn(h4, gn, s0_p, hs_p, l, 512)
        x1 = _merge(xp, oa, ob, g, wpa, wpb, wo, sb_p, tm_p)
        xp, cv_p = _ffn(x1, conv0_p, nf, wup, cw, cb, wdn, nfin, cv_p, l, final, sb_p, tm_p)

        q, k, lat_s, kpe_s, h4, g = _inproj(xs, cos_s, sin_s, nmix, win, qn, wuq, kvn, wuk, wuvt, lbl,
                                            lat_s, kpe_s, l, sb_s, tm_s, False)
        oa = _attn_sample(q, k, lat_s, l, cache_mla_latent[l], cache_mla_krope[l], wuk, wuv)
        ob, hs_s = _hgrn(h4, gn, state_hgrn[l], hs_s, l, 512)
        x1 = _merge(xs, oa, ob, g, wpa, wpb, wo, sb_s, tm_s)
        xs, cv_s = _ffn(x1, state_ffn_conv[l], nf, wup, cw, cb, wdn, nfin, cv_s, l, final, sb_s, tm_s)

    return (xp, xs, lat_p, kpe_p, hs_p, cv_p, lat_s, kpe_s, hs_s, cv_s)
```

```python
import functools
import math

import numpy as np
import jax
import jax.numpy as jnp
from jax import lax
from jax.experimental import pallas as pl
from jax.experimental.pallas import tpu as pltpu

D_MODEL = 1024
CHUNK = 64
EPS = 1e-6
MLA_HEADS = 8
MLA_Q_RANK = 384
MLA_KV_RANK = 256
MLA_NOPE = 128
MLA_ROPE = 64
MLA_V = 128
ROPE_BASE = 10000.0
HG_HEADS = 4
HG_DK = 128
HG_DV = 128
FFN_DIM = 2816
CONV_W = 3

MLA_QK = MLA_NOPE + MLA_ROPE
HG_W = HG_HEADS * HG_DK
QK_PAD = 2 * MLA_NOPE
VT_ROWS = MLA_V + 16
LANES = 128
SUBLANES = 8
VMEM_LIMIT = 56 * 1024 * 1024
ROW_TILE = 512

OFF_CQ = 0
OFF_CKV = OFF_CQ + MLA_Q_RANK
OFF_KR = OFF_CKV + MLA_KV_RANK
OFF_HG = OFF_KR + 2 * MLA_ROPE
OFF_GATE = OFF_HG + 4 * HG_W
IN_COLS_R = OFF_GATE + 2 * D_MODEL

BF16 = jnp.bfloat16
F32 = jnp.float32
NT_DIMS = (((1,), (1,)), ((), ()))
TN_DIMS = (((0,), (0,)), ((), ()))


def _dot(a, b):
    return jnp.dot(a, b, preferred_element_type=F32)


def _dot_nt(a, b):
    return lax.dot_general(a, b, NT_DIMS, preferred_element_type=F32)


def _rms(x, g):
    return x * lax.rsqrt(jnp.mean(x * x, axis=-1, keepdims=True) + EPS) * g


def _sigmoid(x):
    return 0.5 * (jnp.tanh(0.5 * x) + 1.0)


def _resident(shape):
    return pl.BlockSpec(shape, lambda *_: (0,) * len(shape), pipeline_mode=pl.Buffered(1))


def _aliased():
    return pl.BlockSpec(memory_space=pl.ANY)


def _params(*sem):
    return pltpu.CompilerParams(dimension_semantics=sem, vmem_limit_bytes=VMEM_LIMIT)


def _inproj_kernel(x_ref, cos_ref, sin_ref, nmix_ref, win_ref, qn_ref, wuq_ref, kvn_ref, wuk_ref, wuvt_ref, lbl_ref,
                   lat_in, kpe_in, *out_refs, qscale, layer, sb, tm, with_vt):
    del lat_in, kpe_in
    if with_vt:
        q_ref, k_ref, vt_ref, lat_ref, kpe_ref, h4_ref, g_ref = out_refs
    else:
        q_ref, k_ref, lat_ref, kpe_ref, h4_ref, g_ref = out_refs
    rows = sb * tm
    x = x_ref[...].reshape(rows, D_MODEL)
    hb = _rms(x, nmix_ref[...]).astype(BF16)
    cosf = cos_ref[...]
    sinf = sin_ref[...]
    lane = lax.broadcasted_iota(jnp.int32, cosf.shape, 1)
    first_half = (lane % MLA_ROPE) < (MLA_ROPE // 2)
    low_head = lane < MLA_ROPE

    def rot(v):
        sw = jnp.where(first_half, pltpu.roll(v, LANES - MLA_ROPE // 2, 1), pltpu.roll(v, MLA_ROPE // 2, 1))
        return v * cosf + sw * sinf

    cq = _dot(hb, win_ref[:, OFF_CQ:OFF_CKV])
    cqn = _rms(cq, qn_ref[...]).astype(BF16)
    q = _dot(cqn, wuq_ref[...]) * qscale

    ckv = _dot(hb, win_ref[:, OFF_CKV:OFF_KR])
    lat = _rms(ckv, kvn_ref[...])
    lat_ref[0] = lat.reshape(sb, tm, MLA_KV_RANK)
    latb = lat.astype(BF16)
    kn = _dot(latb, wuk_ref[...])
    if with_vt:
        vt = _dot_nt(wuvt_ref[...], latb).astype(BF16)
        ones = jnp.ones((VT_ROWS - MLA_V, rows), BF16)
        for h in range(MLA_HEADS):
            vt_ref[0, h * VT_ROWS:h * VT_ROWS + MLA_V, :] = vt[h * MLA_V:(h + 1) * MLA_V]
            vt_ref[0, h * VT_ROWS + MLA_V:(h + 1) * VT_ROWS, :] = ones

    kpe = rot(_dot(hb, win_ref[:, OFF_KR:OFF_HG]))
    kpe_ref[0] = kpe[:, :MLA_ROPE].reshape(sb, tm, MLA_ROPE)
    kpeb = kpe.astype(BF16)

    nope_w = MLA_HEADS * MLA_NOPE
    for g in range(MLA_HEADS // 2):
        r = rot(q[:, nope_w + g * LANES: nope_w + (g + 1) * LANES])
        for hh in range(2):
            h = 2 * g + hh
            rh = jnp.where(low_head if hh == 0 else jnp.logical_not(low_head), r, 0.0)
            qh = jnp.concatenate([q[:, h * MLA_NOPE:(h + 1) * MLA_NOPE], rh], axis=1).astype(BF16)
            kh = jnp.concatenate([kn[:, h * MLA_NOPE:(h + 1) * MLA_NOPE].astype(BF16), kpeb], axis=1)
            q_ref[:, h] = qh.reshape(sb, tm, QK_PAD)
            k_ref[:, h] = kh.reshape(sb, tm, QK_PAD)

    lg = lbl_ref[...]
    e = jnp.exp(lg - jnp.max(lg, axis=0, keepdims=True))
    sm = e / jnp.sum(e, axis=0, keepdims=True)
    lb = sm[0:1] * 0.0
    for i in range(1, layer + 1):
        lb = lb + sm[i:i + 1]
    h4 = _dot(hb, win_ref[:, OFF_HG:OFF_GATE])
    hq, hf, hi, hg = (h4[:, k * HG_W:(k + 1) * HG_W] for k in range(4))
    h4 = jnp.concatenate([hq * _sigmoid(hq) * HG_DK ** -0.5, lb + (1.0 - lb) * _sigmoid(hf), hi, hg * _sigmoid(hg)],
                         axis=1)
    h4_ref[...] = h4.reshape(sb, tm, 4 * HG_W)
    gates = _sigmoid(_dot(hb, win_ref[:, OFF_GATE:IN_COLS_R]))
    g_ref[...] = gates.astype(BF16).reshape(sb, tm, 2 * D_MODEL)


def _inproj(x, cosf, sinf, nmix, win, qn, wuq, kvn, wuk, wuvt, lbl, lat_buf, kpe_buf, layer, sb, tm, with_vt):
    B, L, _ = x.shape
    H = MLA_HEADS
    qscale = (1.0 / math.sqrt(MLA_QK)) * math.log2(math.e)
    full = _resident
    seq = lambda w: pl.BlockSpec((sb, tm, w), lambda b, t: (b, t, 0))
    stacked = lambda w: pl.BlockSpec((1, sb, tm, w), lambda b, t: (layer, b, t, 0))
    qk_shape = jax.ShapeDtypeStruct((B, H, L, QK_PAD), BF16)
    qk_spec = pl.BlockSpec((sb, H, tm, QK_PAD), lambda b, t: (b, 0, t, 0))
    out_shape = [qk_shape, qk_shape]
    out_specs = [qk_spec, qk_spec]
    if with_vt:
        assert sb == 1
        out_shape.append(jax.ShapeDtypeStruct((B, H * VT_ROWS, L), BF16))
        out_specs.append(pl.BlockSpec((1, H * VT_ROWS, tm), lambda b, t: (b, 0, t)))
    lat_idx = len(out_shape)
    out_shape += [jax.ShapeDtypeStruct(lat_buf.shape, F32), jax.ShapeDtypeStruct(kpe_buf.shape, F32),
                  jax.ShapeDtypeStruct((B, L, 4 * HG_W), F32), jax.ShapeDtypeStruct((B, L, 2 * D_MODEL), BF16)]
    out_specs += [stacked(MLA_KV_RANK), stacked(MLA_ROPE), seq(4 * HG_W), seq(2 * D_MODEL)]
    in_specs = [
        seq(D_MODEL),
        pl.BlockSpec((sb * tm, LANES), lambda b, t: (t, 0)),
        pl.BlockSpec((sb * tm, LANES), lambda b, t: (t, 0)),
        full(nmix.shape), full(win.shape), full(qn.shape), full(wuq.shape), full(kvn.shape),
        full(wuk.shape), full(wuvt.shape), full(lbl.shape),
        _aliased(), _aliased(),
    ]
    return pl.pallas_call(
        functools.partial(_inproj_kernel, qscale=qscale, layer=layer, sb=sb, tm=tm, with_vt=with_vt),
        out_shape=tuple(out_shape), grid=(B // sb, L // tm), in_specs=in_specs, out_specs=tuple(out_specs),
        input_output_aliases={11: lat_idx, 12: lat_idx + 1},
        compiler_params=_params("parallel", "arbitrary"), name="inproj",
    )(x, cosf, sinf, nmix, win, qn, wuq, kvn, wuk, wuvt, lbl, lat_buf, kpe_buf)


ATTN_TK = 512
ATTN_SHIFT_MAX = 64.0


def _attn_kernel(q_ref, k_ref, vt_ref, o_ref, acc_ref, *, seq, tk):
    nst = seq // tk
    kc = lax.broadcasted_iota(jnp.int32, (tk, tk), 0) // CHUNK
    qc = lax.broadcasted_iota(jnp.int32, (tk, tk), 1) // CHUNK
    visible = kc <= qc

    def finish(acc):
        return (acc[:MLA_V] / acc[MLA_V:MLA_V + 1]).T.astype(BF16)

    def scores(j):
        q0 = j * tk
        s = _dot_nt(k_ref[0, 0, q0:q0 + tk, :], q_ref[0, 0, q0:, :])
        sm = jnp.where(visible, s[:, :tk], -jnp.inf)
        return sm if s.shape[1] == tk else jnp.concatenate([sm, s[:, tk:]], axis=1)

    s = scores(0)
    m = jnp.max(s, axis=0, keepdims=True)
    acc_ref[...] = _dot(vt_ref[0, :, 0:tk], jnp.exp2(s - m).astype(BF16))
    excess = jnp.zeros((1, seq), F32)
    for j in range(1, nst):
        q0 = j * tk
        s = scores(j)
        mcur = m[:, q0:]
        p = jnp.exp2(s - mcur).astype(BF16)
        mx = jnp.max(s, axis=0, keepdims=True)
        m_new = jnp.maximum(mcur, mx)
        acc_ref[:, q0:] = jnp.exp2(mcur - m_new) * (acc_ref[:, q0:] + _dot(vt_ref[0, :, q0:q0 + tk], p))
        excess = jnp.concatenate([excess[:, :q0], jnp.maximum(excess[:, q0:], mx - mcur)], axis=1)
        m = jnp.concatenate([m[:, :q0], m_new], axis=1)
    for cb in range(nst):
        o_ref[0, cb * tk:(cb + 1) * tk, :] = finish(acc_ref[:, cb * tk:(cb + 1) * tk])

    @pl.when(jnp.max(excess) > ATTN_SHIFT_MAX)
    def _():
        def column_block(cb, carry):
            c0 = pl.multiple_of(cb * tk, tk)
            qb = q_ref[0, 0, pl.ds(c0, tk), :]
            qcol = (c0 + lax.broadcasted_iota(jnp.int32, (tk, tk), 1)) // CHUNK

            def key_block(j, mla):
                mo, acc = mla
                k0 = pl.multiple_of(j * tk, tk)
                sc = _dot_nt(k_ref[0, 0, pl.ds(k0, tk), :], qb)
                krow = (k0 + lax.broadcasted_iota(jnp.int32, (tk, tk), 0)) // CHUNK
                sc = jnp.where(krow <= qcol, sc, -jnp.inf)
                mn = jnp.maximum(mo, jnp.max(sc, axis=0, keepdims=True))
                pv = _dot(vt_ref[0, :, pl.ds(k0, tk)], jnp.exp2(sc - mn).astype(BF16))
                return mn, jnp.exp2(mo - mn) * acc + pv

            init = (jnp.full((1, tk), -jnp.inf, F32), jnp.zeros((VT_ROWS, tk), F32))
            _, acc = lax.fori_loop(0, cb + 1, key_block, init)
            o_ref[0, pl.ds(c0, tk), :] = finish(acc)
            return carry
        lax.fori_loop(0, nst, column_block, 0)


def _attn_prompt(q, k, vt, tk):
    B, H, L, _ = q.shape
    return pl.pallas_call(
        functools.partial(_attn_kernel, seq=L, tk=tk),
        out_shape=jax.ShapeDtypeStruct((B, L, H * MLA_V), BF16),
        grid=(B, H),
        in_specs=[
            pl.BlockSpec((1, 1, L, QK_PAD), lambda b, h: (b, h, 0, 0)),
            pl.BlockSpec((1, 1, L, QK_PAD), lambda b, h: (b, h, 0, 0)),
            pl.BlockSpec((1, VT_ROWS, L), lambda b, h: (b, h, 0)),
        ],
        out_specs=pl.BlockSpec((1, L, MLA_V), lambda b, h: (b, 0, h)),
        scratch_shapes=[pltpu.VMEM((VT_ROWS, L), F32)],
        compiler_params=_params("parallel", "parallel"), name="attn_prompt",
    )(q, k, vt)


def _attn_sample_kernel(q_ref, kn_ref, latn_ref, latp_ref, kpep_ref, wuk_ref, wuv_ref, o_ref):
    latp = latp_ref[0].astype(BF16)
    latn = latn_ref[0, 0].astype(BF16)
    knp = _dot(latp, wuk_ref[...]).astype(BF16)
    vp = _dot(latp, wuv_ref[...]).astype(BF16)
    vn = _dot(latn, wuv_ref[...]).astype(BF16)
    kp = kpep_ref[0].astype(BF16)
    kpd = jnp.concatenate([kp, kp], axis=1)
    for h in range(MLA_HEADS):
        sl = slice(h * MLA_NOPE, (h + 1) * MLA_NOPE)
        qh = q_ref[0, h]
        khp = jnp.concatenate([knp[:, sl], kpd], axis=1)
        s1 = _dot_nt(qh, khp)
        s2 = _dot_nt(qh, kn_ref[0, h])
        m = jnp.maximum(jnp.max(s1, axis=1, keepdims=True), jnp.max(s2, axis=1, keepdims=True))
        p1 = jnp.exp2(s1 - m)
        p2 = jnp.exp2(s2 - m)
        l = jnp.sum(p1, axis=1, keepdims=True) + jnp.sum(p2, axis=1, keepdims=True)
        o = _dot(p1.astype(BF16), vp[:, sl]) + _dot(p2.astype(BF16), vn[:, sl])
        o_ref[0, :, sl] = (o / l).astype(BF16)


def _attn_sample(q, kn, lat_buf, layer, latp, kpep, wuk, wuv):
    B, H, Ls, _ = q.shape
    P = latp.shape[1]
    full = _resident
    return pl.pallas_call(
        _attn_sample_kernel,
        out_shape=jax.ShapeDtypeStruct((B, Ls, H * MLA_V), BF16),
        grid=(B,),
        in_specs=[
            pl.BlockSpec((1, H, Ls, QK_PAD), lambda b: (b, 0, 0, 0)),
            pl.BlockSpec((1, H, Ls, QK_PAD), lambda b: (b, 0, 0, 0)),
            pl.BlockSpec((1, 1, Ls, MLA_KV_RANK), lambda b: (layer, b, 0, 0)),
            pl.BlockSpec((1, P, MLA_KV_RANK), lambda b: (b, 0, 0)),
            pl.BlockSpec((1, P, MLA_ROPE), lambda b: (b, 0, 0)),
            full(wuk.shape), full(wuv.shape),
        ],
        out_specs=pl.BlockSpec((1, Ls, H * MLA_V), lambda b: (b, 0, 0)),
        compiler_params=_params("parallel"), name="attn_sample",
    )(q, kn, lat_buf, latp, kpep, wuk, wuv)


DIAG_BLOCK = SUBLANES
HGRN_UNROLL = 4


def _hgrn_kernel(h4_ref, gn_ref, s0_ref, sbuf_in, o_ref, sout_ref, st_ref, pb_ref, ob_ref, lvl_ref,
                 *, c, nchunk, unroll):
    del sbuf_in
    t = pl.program_id(1)

    @pl.when(t == 0)
    def _():
        for h in range(HG_HEADS):
            st_ref[h] = s0_ref[0, h].T

    gn = gn_ref[...]

    rt = lax.broadcasted_iota(jnp.int32, (c, c), 0)
    ct = lax.broadcasted_iota(jnp.int32, (c, c), 1)
    tril = jnp.where(ct <= rt, 1.0, 0.0).astype(BF16)
    tril3 = jnp.concatenate([tril, tril, tril], axis=1)
    rt4 = lax.broadcasted_iota(jnp.int32, (HG_HEADS * c, HG_HEADS * c), 0)
    ct4 = lax.broadcasted_iota(jnp.int32, (HG_HEADS * c, HG_HEADS * c), 1)
    lvl = jnp.zeros((HG_HEADS * c, HG_HEADS * c), jnp.int32)
    n, level = c // 2, 0
    while n >= DIAG_BLOCK:
        level += 1
        hit = ((rt4 // (2 * n)) == (ct4 // (2 * n))) & ((rt4 % (2 * n)) >= n) & ((ct4 % (2 * n)) < n)
        lvl = jnp.where(hit, level, lvl)
        n //= 2
    lvl_ref[...] = lvl
    nblk = c // DIAG_BLOCK

    def chunk(ci, u):
        r0 = pl.multiple_of(ci * c, c)

        def seg(k):
            return h4_ref[0, pl.ds(r0, c), pl.ds(k * HG_W, HG_W)]

        def stack(w):
            return jnp.concatenate([w[:, h * HG_DK:(h + 1) * HG_DK] for h in range(HG_HEADS)], axis=0)

        qq_w, f_w, hi_w, gs_w = seg(0), seg(1), seg(2), seg(3)
        logf = jnp.log(f_w)
        kk_w = 1.0 - f_w
        vb_w = hi_w.astype(BF16)

        p0 = logf.astype(BF16)
        r1 = logf - p0.astype(F32)
        p1 = r1.astype(BF16)
        p2 = (r1 - p1.astype(F32)).astype(BF16)
        b_w = _dot(tril3, jnp.concatenate([p0, p1, p2], axis=0))
        bend_w = b_w[c - 1:c, :]
        ebend_w = jnp.exp(bend_w)

        inter = _dot_nt(stack((qq_w * jnp.exp(b_w)).astype(BF16)),
                        st_ref[...].reshape(HG_HEADS * HG_DV, HG_DK).astype(BF16))
        kd_w = (kk_w * jnp.exp(bend_w - b_w)).astype(BF16)
        upd = lax.dot_general(vb_w, kd_w, TN_DIMS, preferred_element_type=F32)
        for h in range(HG_HEADS):
            hs = slice(h * HG_DK, (h + 1) * HG_DK)
            st_ref[h] = st_ref[h] * ebend_w[:, hs] + upd[hs, hs]

        a_all = jnp.zeros((HG_HEADS * c, HG_HEADS * c), F32)
        n = c // 2
        level = 0
        while n >= DIAG_BLOCK:
            qparts, kparts = [], []
            for j in range(c // (2 * n)):
                lo, mid, hi_ = 2 * n * j, 2 * n * j + n, 2 * n * (j + 1)
                ref = b_w[mid - 1:mid, :]
                qparts += [jnp.zeros((n, HG_W), F32), qq_w[mid:hi_] * jnp.exp(b_w[mid:hi_] - ref)]
                kparts += [kk_w[lo:mid] * jnp.exp(ref - b_w[lo:mid]), jnp.zeros((n, HG_W), F32)]
            an = _dot_nt(stack(jnp.concatenate(qparts, axis=0)).astype(BF16),
                         stack(jnp.concatenate(kparts, axis=0)).astype(BF16))
            level += 1
            a_all = jnp.where(lvl_ref[...] == level, an, a_all)
            n //= 2
        intra = _dot(a_all.astype(BF16), stack(vb_w))

        for h in range(HG_HEADS):
            sl = u * HG_HEADS + h
            hs = slice(h * HG_DK, (h + 1) * HG_DK)
            qq, kk, b, hi = qq_w[:, hs], kk_w[:, hs], b_w[:, hs], hi_w[:, hs]
            o = inter[h * c:(h + 1) * c, hs] + intra[h * c:(h + 1) * c]

            pb_ref[sl, 0] = qq
            pb_ref[sl, 1] = kk
            pb_ref[sl, 2] = b
            pb_ref[sl, 3] = hi
            rows = [[pb_ref[sl, k, pl.ds(a, nblk, stride=DIAG_BLOCK), :] for a in range(DIAG_BLOCK)] for k in range(4)]
            for a in range(DIAG_BLOCK):
                oa = jnp.zeros((nblk, HG_DV), F32)
                for e_ in range(a + 1):
                    w = rows[0][a] * rows[1][e_]
                    if e_ < a:
                        w = w * jnp.exp(rows[2][a] - rows[2][e_])
                    oa = oa + jnp.sum(w, axis=1, keepdims=True) * rows[3][e_]
                ob_ref[sl, pl.ds(a, nblk, stride=DIAG_BLOCK), :] = oa
            o = o + ob_ref[sl]

            ob = _rms(o, gn) * gs_w[:, hs]
            o_ref[0, pl.ds(r0, c), pl.ds(h * HG_DV, HG_DV)] = ob.astype(BF16)

    def unrolled(cu, carry):
        for u in range(unroll):
            chunk(cu * unroll + u, u)
        return carry

    lax.fori_loop(0, nchunk // unroll, unrolled, 0)

    @pl.when(t == pl.num_programs(1) - 1)
    def _():
        for h in range(HG_HEADS):
            sout_ref[0, 0, h] = st_ref[h].T


def _hgrn(h4, gnorm, s0, s_buf, layer, tt):
    B, L, _ = h4.shape
    c = min(CHUNK, L)
    tt = min(tt, L)
    nchunk = tt // c
    unroll = HGRN_UNROLL if nchunk % HGRN_UNROLL == 0 else 1
    full = _resident
    return pl.pallas_call(
        functools.partial(_hgrn_kernel, c=c, nchunk=nchunk, unroll=unroll),
        out_shape=(jax.ShapeDtypeStruct((B, L, HG_HEADS * HG_DV), BF16),
                   jax.ShapeDtypeStruct(s_buf.shape, F32)),
        grid=(B, L // tt),
        in_specs=[
            pl.BlockSpec((1, tt, 4 * HG_W), lambda b, t: (b, t, 0)),
            full(gnorm.shape),
            pl.BlockSpec((1, HG_HEADS, HG_DK, HG_DV), lambda b, t: (b, 0, 0, 0)),
            _aliased(),
        ],
        out_specs=(pl.BlockSpec((1, tt, HG_HEADS * HG_DV), lambda b, t: (b, t, 0)),
                   pl.BlockSpec((1, 1, HG_HEADS, HG_DK, HG_DV), lambda b, t: (layer, b, 0, 0, 0))),
        scratch_shapes=[pltpu.VMEM((HG_HEADS, HG_DV, HG_DK), F32),
                        pltpu.VMEM((unroll * HG_HEADS, 4, c, HG_DK), F32),
                        pltpu.VMEM((unroll * HG_HEADS, c, HG_DV), F32),
                        pltpu.VMEM((HG_HEADS * c, HG_HEADS * c), jnp.int32)],
        input_output_aliases={3: 1},
        compiler_params=_params("parallel", "arbitrary"), name="hgrn",
    )(h4, gnorm, s0, s_buf)


def _merge_kernel(x_ref, oa_ref, ob_ref, g_ref, wpa_ref, wpb_ref, wout_ref, y_ref, *, sb, tm):
    rows = sb * tm
    g = g_ref[...].reshape(rows, 2 * D_MODEL).astype(F32)
    oa = oa_ref[...].reshape(rows, MLA_HEADS * MLA_V)
    ob = ob_ref[...].reshape(rows, HG_HEADS * HG_DV)
    mix = g[:, :D_MODEL] * _dot(oa, wpa_ref[...]) + g[:, D_MODEL:] * _dot(ob, wpb_ref[...])
    y = x_ref[...].reshape(rows, D_MODEL) + _dot(mix.astype(BF16), wout_ref[...])
    y_ref[...] = y.reshape(sb, tm, D_MODEL)


def _merge(x, oa, ob, g, wpa, wpb, wout, sb, tm):
    B, L, _ = x.shape
    full = _resident
    row = lambda w: pl.BlockSpec((sb, tm, w), lambda b, t: (b, t, 0))
    return pl.pallas_call(
        functools.partial(_merge_kernel, sb=sb, tm=tm),
        out_shape=jax.ShapeDtypeStruct(x.shape, F32),
        grid=(B // sb, L // tm),
        in_specs=[row(D_MODEL), row(oa.shape[2]), row(ob.shape[2]), row(2 * D_MODEL),
                  full(wpa.shape), full(wpb.shape), full(wout.shape)],
        out_specs=row(D_MODEL),
        compiler_params=_params("parallel", "parallel"), name="merge",
    )(x, oa, ob, g, wpa, wpb, wout)


FFN_SPLIT = 2
FFN_COLS = FFN_DIM // FFN_SPLIT


def _gelu_tanh(x):
    return 0.5 * x * (1.0 + jnp.tanh(math.sqrt(2.0 / math.pi) * (x + 0.044715 * (x * x * x))))


def _ffn_kernel(x_ref, cs_ref, nf_ref, wup_ref, cw_ref, cb_ref, wdn_ref, nfin_ref, cbuf_in, y_ref, cnew_ref,
                carry_ref, *, final, sb, tm):
    del cbuf_in
    t = pl.program_id(1)
    rows = sb * tm

    @pl.when(t == 0)
    def _():
        carry_ref[:, 0:CONV_W - 1, :] = cs_ref[...]

    x = x_ref[...].reshape(rows, D_MODEL)
    hb = _rms(x, nf_ref[...]).astype(BF16)
    pos = lax.broadcasted_iota(jnp.int32, (sb, tm, FFN_COLS), 1).reshape(rows, FFN_COLS)

    def per_row(v):
        return jnp.broadcast_to(v, (sb, tm, FFN_COLS)).reshape(rows, FFN_COLS)

    acc = x
    for c in range(FFN_SPLIT):
        cs = slice(c * FFN_COLS, (c + 1) * FFN_COLS)
        a = _dot(hb, wup_ref[:, c * FFN_COLS:(c + 1) * FFN_COLS])
        val = _dot(hb, wup_ref[:, FFN_DIM + c * FFN_COLS:FFN_DIM + (c + 1) * FFN_COLS])
        prev2 = per_row(carry_ref[:, 0:1, cs])
        prev1 = per_row(carry_ref[:, 1:2, cs])
        a1 = jnp.where(pos == 0, prev1, pltpu.roll(a, 1, 0))
        a2 = jnp.where(pos == 0, prev2, jnp.where(pos == 1, prev1, pltpu.roll(a, 2, 0)))
        conv = cb_ref[:, cs] + cw_ref[0:1, cs] * a2 + cw_ref[1:2, cs] * a1 + cw_ref[2:3, cs] * a
        gated = (_gelu_tanh(conv) * val).astype(BF16)
        acc = acc + _dot(gated, wdn_ref[c * FFN_COLS:(c + 1) * FFN_COLS, :])
        carry_ref[:, 0:CONV_W - 1, cs] = a.reshape(sb, tm, FFN_COLS)[:, tm - (CONV_W - 1):tm, :]
    if final:
        acc = _rms(acc, nfin_ref[...])
    y_ref[...] = acc.reshape(sb, tm, D_MODEL)

    @pl.when(t == pl.num_programs(1) - 1)
    def _():
        cnew_ref[0] = carry_ref[:, 0:CONV_W - 1, :]


def _ffn(x, conv_state, nf, wup, cw, cb, wdn, nfin, c_buf, layer, final, sb, tm):
    B, L, _ = x.shape
    full = _resident
    return pl.pallas_call(
        functools.partial(_ffn_kernel, final=final, sb=sb, tm=tm),
        out_shape=(jax.ShapeDtypeStruct(x.shape, F32), jax.ShapeDtypeStruct(c_buf.shape, F32)),
        grid=(B // sb, L // tm),
        in_specs=[
            pl.BlockSpec((sb, tm, D_MODEL), lambda b, t: (b, t, 0)),
            pl.BlockSpec((sb, CONV_W - 1, FFN_DIM), lambda b, t: (b, 0, 0)),
            full(nf.shape), full(wup.shape), full(cw.shape), full(cb.shape), full(wdn.shape), full(nfin.shape),
            _aliased(),
        ],
        out_specs=(pl.BlockSpec((sb, tm, D_MODEL), lambda b, t: (b, t, 0)),
                   pl.BlockSpec((1, sb, CONV_W - 1, FFN_DIM), lambda b, t: (layer, b, 0, 0))),
        scratch_shapes=[pltpu.VMEM((sb, SUBLANES, FFN_DIM), F32)],
        input_output_aliases={8: 1},
        compiler_params=_params("parallel", "arbitrary"), name="ffn",
    )(x, conv_state, nf, wup, cw, cb, wdn, nfin, c_buf)


def _rope_tables(pos):
    half = MLA_ROPE // 2
    inv = ROPE_BASE ** (-jnp.arange(half, dtype=F32) / half)
    ang = pos.astype(F32)[:, None] * inv[None, :]
    cos, sin = jnp.cos(ang), jnp.sin(ang)
    return jnp.tile(cos, (1, 4)), jnp.concatenate([-sin, sin, -sin, sin], axis=1)


def _prep_layer(w_in_l, w_uq_l, w_ukv_l):
    kr = w_in_l[:, OFF_KR:OFF_KR + MLA_ROPE]
    win = jnp.concatenate([w_in_l[:, :OFF_KR], kr, kr, w_in_l[:, OFF_KR + MLA_ROPE:]], axis=1).astype(BF16)
    uq = w_uq_l.reshape(MLA_Q_RANK, MLA_HEADS, MLA_QK)
    wuq = jnp.concatenate([uq[:, :, :MLA_NOPE].reshape(MLA_Q_RANK, -1),
                           uq[:, :, MLA_NOPE:].reshape(MLA_Q_RANK, -1)], axis=1).astype(BF16)
    ukv = w_ukv_l.reshape(MLA_KV_RANK, MLA_HEADS, MLA_NOPE + MLA_V)
    wuk = ukv[:, :, :MLA_NOPE].reshape(MLA_KV_RANK, -1).astype(BF16)
    wuv = ukv[:, :, MLA_NOPE:].reshape(MLA_KV_RANK, -1).astype(BF16)
    return win, wuq, wuk, wuv


def _tiling(B, L):
    if L % ROW_TILE == 0:
        return 1, ROW_TILE
    sb = 1
    if L % SUBLANES == 0:
        for cand in range(1, B + 1):
            if B % cand == 0 and cand * L <= ROW_TILE:
                sb = cand
    return sb, L


def _state_buffers(depth, B, L):
    return (jnp.zeros((depth, B, L, MLA_KV_RANK), F32), jnp.zeros((depth, B, L, MLA_ROPE), F32),
            jnp.zeros((depth, B, HG_HEADS, HG_DK, HG_DV), F32), jnp.zeros((depth, B, CONV_W - 1, FFN_DIM), F32))


def kernel(x_prompt, x_sample, cache_mla_latent, cache_mla_krope, state_hgrn, state_ffn_conv, norm_mix, w_in, q_norm, w_uq, kv_norm, w_ukv, lb_logits, hgrn_norm, w_proj_a, w_proj_b, w_out, norm_ffn, w_up, conv_w, conv_b, w_down, norm_final):
    depth = w_in.shape[0]
    B, Lp, _ = x_prompt.shape
    Bs, Ls, _ = x_sample.shape
    past = cache_mla_latent.shape[2]

    sb_p, tm_p = _tiling(B, Lp)
    sb_s, tm_s = _tiling(Bs, Ls)
    cos_p, sin_p = _rope_tables(jnp.arange(Lp))
    cos_s, sin_s = _rope_tables(past + jnp.arange(Ls))
    cos_s, sin_s = jnp.tile(cos_s, (sb_s, 1)), jnp.tile(sin_s, (sb_s, 1))
    s0_p = jnp.zeros((B, HG_HEADS, HG_DK, HG_DV), F32)
    conv0_p = jnp.zeros((B, CONV_W - 1, FFN_DIM), F32)
    nfin = norm_final.reshape(1, D_MODEL)
    lbl = lb_logits.astype(F32)

    tk = ATTN_TK if Lp % ATTN_TK == 0 else Lp

    lat_p, kpe_p, hs_p, cv_p = _state_buffers(depth, B, Lp)
    lat_s, kpe_s, hs_s, cv_s = _state_buffers(depth, Bs, Ls)

    xp, xs = x_prompt, x_sample
    for l in range(depth):
        win, wuq, wuk, wuv = _prep_layer(w_in[l], w_uq[l], w_ukv[l])
        wuvt = wuv.T
        nmix = norm_mix[l].reshape(1, -1)
        qn = q_norm[l].reshape(1, -1)
        kvn = kv_norm[l].reshape(1, -1)
        gn = hgrn_norm[l].reshape(1, -1)
        nf = norm_ffn[l].reshape(1, -1)
        wpa, wpb, wo = w_proj_a[l].astype(BF16), w_proj_b[l].astype(BF16), w_out[l].astype(BF16)
        wup, wdn = w_up[l].astype(BF16), w_down[l].astype(BF16)
        cw, cb = conv_w[l], conv_b[l].reshape(1, -1)
        final = l == depth - 1

        q, k, vt, lat_p, kpe_p, h4, g = _inproj(xp, cos_p, sin_p, nmix, win, qn, wuq, kvn, wuk, wuvt, lbl,
                                                lat_p, kpe_p, l, sb_p, tm_p, True)
        oa = _attn_prompt(q, k, vt, tk)
        ob, hs_p = _hgrn(h4, gn, s0_p, hs_p, l, 512)
        x1 = _merge(xp, oa, ob, g, wpa, wpb, wo, sb_p, tm_p)
        xp, cv_p = _ffn(x1, conv0_p, nf, wup, cw, cb, wdn, nfin, cv_p, l, final, sb_p, tm_p)

        q, k, lat_s, kpe_s, h4, g = _inproj(xs, cos_s, sin_s, nmix, win, qn, wuq, kvn, wuk, wuvt, lbl,
                                            lat_s, kpe_s, l, sb_s, tm_s, False)
        oa = _attn_sample(q, k, lat_s, l, cache_mla_latent[l], cache_mla_krope[l], wuk, wuv)
        ob, hs_s = _hgrn(h4, gn, state_hgrn[l], hs_s, l, 512)
        x1 = _merge(xs, oa, ob, g, wpa, wpb, wo, sb_s, tm_s)
        xs, cv_s = _ffn(x1, state_ffn_conv[l], nf, wup, cw, cb, wdn, nfin, cv_s, l, final, sb_s, tm_s)

    return (xp, xs, lat_p, kpe_p, hs_p, cv_p, lat_s, kpe_s, hs_s, cv_s)
```

```python
import functools
import math

import numpy as np
import jax
import jax.numpy as jnp
from jax import lax
from jax.experimental import pallas as pl
from jax.experimental.pallas import tpu as pltpu

D_MODEL = 1024
CHUNK = 64
EPS = 1e-6
MLA_HEADS = 8
MLA_Q_RANK = 384
MLA_KV_RANK = 256
MLA_NOPE = 128
MLA_ROPE = 64
MLA_V = 128
ROPE_BASE = 10000.0
HG_HEADS = 4
HG_DK = 128
HG_DV = 128
FFN_DIM = 2816
CONV_W = 3

MLA_QK = MLA_NOPE + MLA_ROPE
HG_W = HG_HEADS * HG_DK
QK_PAD = 2 * MLA_NOPE
VT_ROWS = MLA_V + 16
LANES = 128
SUBLANES = 8
VMEM_LIMIT = 56 * 1024 * 1024
ROW_TILE = 512

OFF_CQ = 0
OFF_CKV = OFF_CQ + MLA_Q_RANK
OFF_KR = OFF_CKV + MLA_KV_RANK
OFF_HG = OFF_KR + 2 * MLA_ROPE
OFF_GATE = OFF_HG + 4 * HG_W
IN_COLS_R = OFF_GATE + 2 * D_MODEL

BF16 = jnp.bfloat16
F32 = jnp.float32
NT_DIMS = (((1,), (1,)), ((), ()))
TN_DIMS = (((0,), (0,)), ((), ()))


def _dot(a, b):
    return jnp.dot(a, b, preferred_element_type=F32)


def _dot_nt(a, b):
    return lax.dot_general(a, b, NT_DIMS, preferred_element_type=F32)


def _rms(x, g):
    return x * lax.rsqrt(jnp.mean(x * x, axis=-1, keepdims=True) + EPS) * g


def _sigmoid(x):
    return 0.5 * (jnp.tanh(0.5 * x) + 1.0)


def _resident(shape):
    return pl.BlockSpec(shape, lambda *_: (0,) * len(shape), pipeline_mode=pl.Buffered(1))


def _aliased():
    return pl.BlockSpec(memory_space=pl.ANY)


def _params(*sem):
    return pltpu.CompilerParams(dimension_semantics=sem, vmem_limit_bytes=VMEM_LIMIT)


def _inproj_kernel(x_ref, cos_ref, sin_ref, nmix_ref, win_ref, qn_ref, wuq_ref, kvn_ref, wuk_ref, wuvt_ref, lbl_ref,
                   lat_in, kpe_in, *out_refs, qscale, layer, sb, tm, with_vt):
    del lat_in, kpe_in
    if with_vt:
        q_ref, k_ref, vt_ref, lat_ref, kpe_ref, h4_ref, g_ref = out_refs
    else:
        q_ref, k_ref, lat_ref, kpe_ref, h4_ref, g_ref = out_refs
    rows = sb * tm
    x = x_ref[...].reshape(rows, D_MODEL)
    hb = _rms(x, nmix_ref[...]).astype(BF16)
    cosf = cos_ref[...]
    sinf = sin_ref[...]
    lane = lax.broadcasted_iota(jnp.int32, cosf.shape, 1)
    first_half = (lane % MLA_ROPE) < (MLA_ROPE // 2)
    low_head = lane < MLA_ROPE

    def rot(v):
        sw = jnp.where(first_half, pltpu.roll(v, LANES - MLA_ROPE // 2, 1), pltpu.roll(v, MLA_ROPE // 2, 1))
        return v * cosf + sw * sinf

    cq = _dot(hb, win_ref[:, OFF_CQ:OFF_CKV])
    ckv = _dot(hb, win_ref[:, OFF_CKV:OFF_KR])
    kr = _dot(hb, win_ref[:, OFF_KR:OFF_HG])
    h4 = _dot(hb, win_ref[:, OFF_HG:OFF_GATE])
    cqn = _rms(cq, qn_ref[...]).astype(BF16)
    lat = _rms(ckv, kvn_ref[...])
    lat_ref[0] = lat.reshape(sb, tm, MLA_KV_RANK)
    latb = lat.astype(BF16)
    q = _dot(cqn, wuq_ref[...]) * qscale
    kn = _dot(latb, wuk_ref[...])
    if with_vt:
        vt = _dot_nt(wuvt_ref[...], latb).astype(BF16)
        ones = jnp.ones((VT_ROWS - MLA_V, rows), BF16)
        for h in range(MLA_HEADS):
            vt_ref[0, h * VT_ROWS:h * VT_ROWS + MLA_V, :] = vt[h * MLA_V:(h + 1) * MLA_V]
            vt_ref[0, h * VT_ROWS + MLA_V:(h + 1) * VT_ROWS, :] = ones
    graw = _dot(hb, win_ref[:, OFF_GATE:IN_COLS_R])

    kpe = rot(kr)
    kpe_ref[0] = kpe[:, :MLA_ROPE].reshape(sb, tm, MLA_ROPE)
    kpeb = kpe.astype(BF16)

    nope_w = MLA_HEADS * MLA_NOPE
    for g in range(MLA_HEADS // 2):
        r = rot(q[:, nope_w + g * LANES: nope_w + (g + 1) * LANES])
        for hh in range(2):
            h = 2 * g + hh
            rh = jnp.where(low_head if hh == 0 else jnp.logical_not(low_head), r, 0.0)
            qh = jnp.concatenate([q[:, h * MLA_NOPE:(h + 1) * MLA_NOPE], rh], axis=1).astype(BF16)
            kh = jnp.concatenate([kn[:, h * MLA_NOPE:(h + 1) * MLA_NOPE].astype(BF16), kpeb], axis=1)
            q_ref[:, h] = qh.reshape(sb, tm, QK_PAD)
            k_ref[:, h] = kh.reshape(sb, tm, QK_PAD)

    lg = lbl_ref[...]
    e = jnp.exp(lg - jnp.max(lg, axis=0, keepdims=True))
    sm = e / jnp.sum(e, axis=0, keepdims=True)
    lb = sm[0:1] * 0.0
    for i in range(1, layer + 1):
        lb = lb + sm[i:i + 1]
    hq, hf, hi, hg = (h4[:, k * HG_W:(k + 1) * HG_W] for k in range(4))
    h4 = jnp.concatenate([hq * _sigmoid(hq) * HG_DK ** -0.5, lb + (1.0 - lb) * _sigmoid(hf), hi, hg * _sigmoid(hg)],
                         axis=1)
    h4_ref[...] = h4.reshape(sb, tm, 4 * HG_W)
    gates = _sigmoid(graw)
    g_ref[...] = gates.astype(BF16).reshape(sb, tm, 2 * D_MODEL)


def _inproj(x, cosf, sinf, nmix, win, qn, wuq, kvn, wuk, wuvt, lbl, lat_buf, kpe_buf, layer, sb, tm, with_vt):
    B, L, _ = x.shape
    H = MLA_HEADS
    qscale = (1.0 / math.sqrt(MLA_QK)) * math.log2(math.e)
    full = _resident
    seq = lambda w: pl.BlockSpec((sb, tm, w), lambda b, t: (b, t, 0))
    stacked = lambda w: pl.BlockSpec((1, sb, tm, w), lambda b, t: (layer, b, t, 0))
    qk_shape = jax.ShapeDtypeStruct((B, H, L, QK_PAD), BF16)
    qk_spec = pl.BlockSpec((sb, H, tm, QK_PAD), lambda b, t: (b, 0, t, 0))
    out_shape = [qk_shape, qk_shape]
    out_specs = [qk_spec, qk_spec]
    if with_vt:
        assert sb == 1
        out_shape.append(jax.ShapeDtypeStruct((B, H * VT_ROWS, L), BF16))
        out_specs.append(pl.BlockSpec((1, H * VT_ROWS, tm), lambda b, t: (b, 0, t)))
    lat_idx = len(out_shape)
    out_shape += [jax.ShapeDtypeStruct(lat_buf.shape, F32), jax.ShapeDtypeStruct(kpe_buf.shape, F32),
                  jax.ShapeDtypeStruct((B, L, 4 * HG_W), F32), jax.ShapeDtypeStruct((B, L, 2 * D_MODEL), BF16)]
    out_specs += [stacked(MLA_KV_RANK), stacked(MLA_ROPE), seq(4 * HG_W), seq(2 * D_MODEL)]
    in_specs = [
        seq(D_MODEL),
        pl.BlockSpec((sb * tm, LANES), lambda b, t: (t, 0)),
        pl.BlockSpec((sb * tm, LANES), lambda b, t: (t, 0)),
        full(nmix.shape), full(win.shape), full(qn.shape), full(wuq.shape), full(kvn.shape),
        full(wuk.shape), full(wuvt.shape), full(lbl.shape),
        _aliased(), _aliased(),
    ]
    return pl.pallas_call(
        functools.partial(_inproj_kernel, qscale=qscale, layer=layer, sb=sb, tm=tm, with_vt=with_vt),
        out_shape=tuple(out_shape), grid=(B // sb, L // tm), in_specs=in_specs, out_specs=tuple(out_specs),
        input_output_aliases={11: lat_idx, 12: lat_idx + 1},
        compiler_params=_params("parallel", "arbitrary"), name="inproj",
    )(x, cosf, sinf, nmix, win, qn, wuq, kvn, wuk, wuvt, lbl, lat_buf, kpe_buf)


ATTN_TK = 512
ATTN_SHIFT_MAX = 64.0


def _attn_kernel(q_ref, k_ref, vt_ref, o_ref, acc_ref, *, seq, tk):
    nst = seq // tk
    kc = lax.broadcasted_iota(jnp.int32, (tk, tk), 0) // CHUNK
    qc = lax.broadcasted_iota(jnp.int32, (tk, tk), 1) // CHUNK
    visible = kc <= qc

    def finish(acc):
        return (acc[:MLA_V] / acc[MLA_V:MLA_V + 1]).T.astype(BF16)

    def scores(j):
        q0 = j * tk
        s = _dot_nt(k_ref[0, 0, q0:q0 + tk, :], q_ref[0, 0, q0:, :])
        sm = jnp.where(visible, s[:, :tk], -jnp.inf)
        return sm if s.shape[1] == tk else jnp.concatenate([sm, s[:, tk:]], axis=1)

    s = scores(0)
    m = jnp.max(s, axis=0, keepdims=True)
    acc_ref[...] = _dot(vt_ref[0, :, 0:tk], jnp.exp2(s - m).astype(BF16))
    excess = jnp.zeros((1, seq), F32)
    for j in range(1, nst):
        q0 = j * tk
        s = scores(j)
        mcur = m[:, q0:]
        p = jnp.exp2(s - mcur).astype(BF16)
        mx = jnp.max(s, axis=0, keepdims=True)
        m_new = jnp.maximum(mcur, mx)
        acc_ref[:, q0:] = jnp.exp2(mcur - m_new) * (acc_ref[:, q0:] + _dot(vt_ref[0, :, q0:q0 + tk], p))
        excess = jnp.concatenate([excess[:, :q0], jnp.maximum(excess[:, q0:], mx - mcur)], axis=1)
        m = jnp.concatenate([m[:, :q0], m_new], axis=1)
    for cb in range(nst):
        o_ref[0, cb * tk:(cb + 1) * tk, :] = finish(acc_ref[:, cb * tk:(cb + 1) * tk])

    @pl.when(jnp.max(excess) > ATTN_SHIFT_MAX)
    def _():
        def column_block(cb, carry):
            c0 = pl.multiple_of(cb * tk, tk)
            qb = q_ref[0, 0, pl.ds(c0, tk), :]
            qcol = (c0 + lax.broadcasted_iota(jnp.int32, (tk, tk), 1)) // CHUNK

            def key_block(j, mla):
                mo, acc = mla
                k0 = pl.multiple_of(j * tk, tk)
                sc = _dot_nt(k_ref[0, 0, pl.ds(k0, tk), :], qb)
                krow = (k0 + lax.broadcasted_iota(jnp.int32, (tk, tk), 0)) // CHUNK
                sc = jnp.where(krow <= qcol, sc, -jnp.inf)
                mn = jnp.maximum(mo, jnp.max(sc, axis=0, keepdims=True))
                pv = _dot(vt_ref[0, :, pl.ds(k0, tk)], jnp.exp2(sc - mn).astype(BF16))
                return mn, jnp.exp2(mo - mn) * acc + pv

            init = (jnp.full((1, tk), -jnp.inf, F32), jnp.zeros((VT_ROWS, tk), F32))
            _, acc = lax.fori_loop(0, cb + 1, key_block, init)
            o_ref[0, pl.ds(c0, tk), :] = finish(acc)
            return carry
        lax.fori_loop(0, nst, column_block, 0)


def _attn_prompt(q, k, vt, tk):
    B, H, L, _ = q.shape
    return pl.pallas_call(
        functools.partial(_attn_kernel, seq=L, tk=tk),
        out_shape=jax.ShapeDtypeStruct((B, L, H * MLA_V), BF16),
        grid=(B, H),
        in_specs=[
            pl.BlockSpec((1, 1, L, QK_PAD), lambda b, h: (b, h, 0, 0)),
            pl.BlockSpec((1, 1, L, QK_PAD), lambda b, h: (b, h, 0, 0)),
            pl.BlockSpec((1, VT_ROWS, L), lambda b, h: (b, h, 0)),
        ],
        out_specs=pl.BlockSpec((1, L, MLA_V), lambda b, h: (b, 0, h)),
        scratch_shapes=[pltpu.VMEM((VT_ROWS, L), F32)],
        compiler_params=_params("parallel", "parallel"), name="attn_prompt",
    )(q, k, vt)


def _attn_sample_kernel(q_ref, kn_ref, latn_ref, latp_ref, kpep_ref, wuk_ref, wuv_ref, o_ref):
    latp = latp_ref[0].astype(BF16)
    latn = latn_ref[0, 0].astype(BF16)
    knp = _dot(latp, wuk_ref[...]).astype(BF16)
    vp = _dot(latp, wuv_ref[...]).astype(BF16)
    vn = _dot(latn, wuv_ref[...]).astype(BF16)
    kp = kpep_ref[0].astype(BF16)
    kpd = jnp.concatenate([kp, kp], axis=1)
    for h in range(MLA_HEADS):
        sl = slice(h * MLA_NOPE, (h + 1) * MLA_NOPE)
        qh = q_ref[0, h]
        khp = jnp.concatenate([knp[:, sl], kpd], axis=1)
        s1 = _dot_nt(qh, khp)
        s2 = _dot_nt(qh, kn_ref[0, h])
        m = jnp.maximum(jnp.max(s1, axis=1, keepdims=True), jnp.max(s2, axis=1, keepdims=True))
        p1 = jnp.exp2(s1 - m)
        p2 = jnp.exp2(s2 - m)
        l = jnp.sum(p1, axis=1, keepdims=True) + jnp.sum(p2, axis=1, keepdims=True)
        o = _dot(p1.astype(BF16), vp[:, sl]) + _dot(p2.astype(BF16), vn[:, sl])
        o_ref[0, :, sl] = (o / l).astype(BF16)


def _attn_sample(q, kn, lat_buf, layer, latp, kpep, wuk, wuv):
    B, H, Ls, _ = q.shape
    P = latp.shape[1]
    full = _resident
    return pl.pallas_call(
        _attn_sample_kernel,
        out_shape=jax.ShapeDtypeStruct((B, Ls, H * MLA_V), BF16),
        grid=(B,),
        in_specs=[
            pl.BlockSpec((1, H, Ls, QK_PAD), lambda b: (b, 0, 0, 0)),
            pl.BlockSpec((1, H, Ls, QK_PAD), lambda b: (b, 0, 0, 0)),
            pl.BlockSpec((1, 1, Ls, MLA_KV_RANK), lambda b: (layer, b, 0, 0)),
            pl.BlockSpec((1, P, MLA_KV_RANK), lambda b: (b, 0, 0)),
            pl.BlockSpec((1, P, MLA_ROPE), lambda b: (b, 0, 0)),
            full(wuk.shape), full(wuv.shape),
        ],
        out_specs=pl.BlockSpec((1, Ls, H * MLA_V), lambda b: (b, 0, 0)),
        compiler_params=_params("parallel"), name="attn_sample",
    )(q, kn, lat_buf, latp, kpep, wuk, wuv)


DIAG_BLOCK = SUBLANES
HGRN_UNROLL = 4


def _hgrn_kernel(h4_ref, gn_ref, s0_ref, sbuf_in, o_ref, sout_ref, st_ref, pb_ref, ob_ref, lvl_ref,
                 *, c, nchunk, unroll):
    del sbuf_in
    t = pl.program_id(1)

    @pl.when(t == 0)
    def _():
        for h in range(HG_HEADS):
            st_ref[h] = s0_ref[0, h].T

    gn = gn_ref[...]

    rt = lax.broadcasted_iota(jnp.int32, (c, c), 0)
    ct = lax.broadcasted_iota(jnp.int32, (c, c), 1)
    tril = jnp.where(ct <= rt, 1.0, 0.0).astype(BF16)
    tril3 = jnp.concatenate([tril, tril, tril], axis=1)
    rt4 = lax.broadcasted_iota(jnp.int32, (HG_HEADS * c, HG_HEADS * c), 0)
    ct4 = lax.broadcasted_iota(jnp.int32, (HG_HEADS * c, HG_HEADS * c), 1)
    lvl = jnp.zeros((HG_HEADS * c, HG_HEADS * c), jnp.int32)
    n, level = c // 2, 0
    while n >= DIAG_BLOCK:
        level += 1
        hit = ((rt4 // (2 * n)) == (ct4 // (2 * n))) & ((rt4 % (2 * n)) >= n) & ((ct4 % (2 * n)) < n)
        lvl = jnp.where(hit, level, lvl)
        n //= 2
    lvl_ref[...] = lvl
    nblk = c // DIAG_BLOCK

    def chunk(ci, u):
        r0 = pl.multiple_of(ci * c, c)

        def seg(k):
            return h4_ref[0, pl.ds(r0, c), pl.ds(k * HG_W, HG_W)]

        def stack(w):
            return jnp.concatenate([w[:, h * HG_DK:(h + 1) * HG_DK] for h in range(HG_HEADS)], axis=0)

        qq_w, f_w, hi_w, gs_w = seg(0), seg(1), seg(2), seg(3)
        logf = jnp.log(f_w)
        kk_w = 1.0 - f_w
        vb_w = hi_w.astype(BF16)

        p0 = logf.astype(BF16)
        r1 = logf - p0.astype(F32)
        p1 = r1.astype(BF16)
        p2 = (r1 - p1.astype(F32)).astype(BF16)
        b_w = _dot(tril3, jnp.concatenate([p0, p1, p2], axis=0))
        bend_w = b_w[c - 1:c, :]
        ebend_w = jnp.exp(bend_w)

        inter = _dot_nt(stack((qq_w * jnp.exp(b_w)).astype(BF16)),
                        st_ref[...].reshape(HG_HEADS * HG_DV, HG_DK).astype(BF16))
        kd_w = (kk_w * jnp.exp(bend_w - b_w)).astype(BF16)
        upd = lax.dot_general(vb_w, kd_w, TN_DIMS, preferred_element_type=F32)
        for h in range(HG_HEADS):
            hs = slice(h * HG_DK, (h + 1) * HG_DK)
            st_ref[h] = st_ref[h] * ebend_w[:, hs] + upd[hs, hs]

        a_all = jnp.zeros((HG_HEADS * c, HG_HEADS * c), F32)
        n = c // 2
        level = 0
        while n >= DIAG_BLOCK:
            qparts, kparts = [], []
            for j in range(c // (2 * n)):
                lo, mid, hi_ = 2 * n * j, 2 * n * j + n, 2 * n * (j + 1)
                ref = b_w[mid - 1:mid, :]
                qparts += [jnp.zeros((n, HG_W), F32), qq_w[mid:hi_] * jnp.exp(b_w[mid:hi_] - ref)]
                kparts += [kk_w[lo:mid] * jnp.exp(ref - b_w[lo:mid]), jnp.zeros((n, HG_W), F32)]
            an = _dot_nt(stack(jnp.concatenate(qparts, axis=0)).astype(BF16),
                         stack(jnp.concatenate(kparts, axis=0)).astype(BF16))
            level += 1
            a_all = jnp.where(lvl_ref[...] == level, an, a_all)
            n //= 2
        intra = _dot(a_all.astype(BF16), stack(vb_w))

        for h in range(HG_HEADS):
            sl = u * HG_HEADS + h
            hs = slice(h * HG_DK, (h + 1) * HG_DK)
            qq, kk, b, hi = qq_w[:, hs], kk_w[:, hs], b_w[:, hs], hi_w[:, hs]
            o = inter[h * c:(h + 1) * c, hs] + intra[h * c:(h + 1) * c]

            pb_ref[sl, 0] = qq
            pb_ref[sl, 1] = kk
            pb_ref[sl, 2] = b
            pb_ref[sl, 3] = hi
            rows = [[pb_ref[sl, k, pl.ds(a, nblk, stride=DIAG_BLOCK), :] for a in range(DIAG_BLOCK)] for k in range(4)]
            for a in range(DIAG_BLOCK):
                oa = jnp.zeros((nblk, HG_DV), F32)
                for e_ in range(a + 1):
                    w = rows[0][a] * rows[1][e_]
                    if e_ < a:
                        w = w * jnp.exp(rows[2][a] - rows[2][e_])
                    oa = oa + jnp.sum(w, axis=1, keepdims=True) * rows[3][e_]
                ob_ref[sl, pl.ds(a, nblk, stride=DIAG_BLOCK), :] = oa
            o = o + ob_ref[sl]

            ob = _rms(o, gn) * gs_w[:, hs]
            o_ref[0, pl.ds(r0, c), pl.ds(h * HG_DV, HG_DV)] = ob.astype(BF16)

    def unrolled(cu, carry):
        for u in range(unroll):
            chunk(cu * unroll + u, u)
        return carry

    lax.fori_loop(0, nchunk // unroll, unrolled, 0)

    @pl.when(t == pl.num_programs(1) - 1)
    def _():
        for h in range(HG_HEADS):
            sout_ref[0, 0, h] = st_ref[h].T


def _hgrn(h4, gnorm, s0, s_buf, layer, tt):
    B, L, _ = h4.shape
    c = min(CHUNK, L)
    tt = min(tt, L)
    nchunk = tt // c
    unroll = HGRN_UNROLL if nchunk % HGRN_UNROLL == 0 else 1
    full = _resident
    return pl.pallas_call(
        functools.partial(_hgrn_kernel, c=c, nchunk=nchunk, unroll=unroll),
        out_shape=(jax.ShapeDtypeStruct((B, L, HG_HEADS * HG_DV), BF16),
                   jax.ShapeDtypeStruct(s_buf.shape, F32)),
        grid=(B, L // tt),
        in_specs=[
            pl.BlockSpec((1, tt, 4 * HG_W), lambda b, t: (b, t, 0)),
            full(gnorm.shape),
            pl.BlockSpec((1, HG_HEADS, HG_DK, HG_DV), lambda b, t: (b, 0, 0, 0)),
            _aliased(),
        ],
        out_specs=(pl.BlockSpec((1, tt, HG_HEADS * HG_DV), lambda b, t: (b, t, 0)),
                   pl.BlockSpec((1, 1, HG_HEADS, HG_DK, HG_DV), lambda b, t: (layer, b, 0, 0, 0))),
        scratch_shapes=[pltpu.VMEM((HG_HEADS, HG_DV, HG_DK), F32),
                        pltpu.VMEM((unroll * HG_HEADS, 4, c, HG_DK), F32),
                        pltpu.VMEM((unroll * HG_HEADS, c, HG_DV), F32),
                        pltpu.VMEM((HG_HEADS * c, HG_HEADS * c), jnp.int32)],
        input_output_aliases={3: 1},
        compiler_params=_params("parallel", "arbitrary"), name="hgrn",
    )(h4, gnorm, s0, s_buf)


def _merge_kernel(x_ref, oa_ref, ob_ref, g_ref, wpa_ref, wpb_ref, wout_ref, y_ref, *, sb, tm):
    rows = sb * tm
    g = g_ref[...].reshape(rows, 2 * D_MODEL).astype(F32)
    oa = oa_ref[...].reshape(rows, MLA_HEADS * MLA_V)
    ob = ob_ref[...].reshape(rows, HG_HEADS * HG_DV)
    mix = g[:, :D_MODEL] * _dot(oa, wpa_ref[...]) + g[:, D_MODEL:] * _dot(ob, wpb_ref[...])
    y = x_ref[...].reshape(rows, D_MODEL) + _dot(mix.astype(BF16), wout_ref[...])
    y_ref[...] = y.reshape(sb, tm, D_MODEL)


def _merge(x, oa, ob, g, wpa, wpb, wout, sb, tm):
    B, L, _ = x.shape
    full = _resident
    row = lambda w: pl.BlockSpec((sb, tm, w), lambda b, t: (b, t, 0))
    return pl.pallas_call(
        functools.partial(_merge_kernel, sb=sb, tm=tm),
        out_shape=jax.ShapeDtypeStruct(x.shape, F32),
        grid=(B // sb, L // tm),
        in_specs=[row(D_MODEL), row(oa.shape[2]), row(ob.shape[2]), row(2 * D_MODEL),
                  full(wpa.shape), full(wpb.shape), full(wout.shape)],
        out_specs=row(D_MODEL),
        compiler_params=_params("parallel", "parallel"), name="merge",
    )(x, oa, ob, g, wpa, wpb, wout)


FFN_SPLIT = 2
FFN_COLS = FFN_DIM // FFN_SPLIT


def _gelu_tanh(x):
    return 0.5 * x * (1.0 + jnp.tanh(math.sqrt(2.0 / math.pi) * (x + 0.044715 * (x * x * x))))


def _ffn_kernel(x_ref, cs_ref, nf_ref, wup_ref, cw_ref, cb_ref, wdn_ref, nfin_ref, cbuf_in, y_ref, cnew_ref,
                carry_ref, *, final, sb, tm):
    del cbuf_in
    t = pl.program_id(1)
    rows = sb * tm

    @pl.when(t == 0)
    def _():
        carry_ref[:, 0:CONV_W - 1, :] = cs_ref[...]

    x = x_ref[...].reshape(rows, D_MODEL)
    hb = _rms(x, nf_ref[...]).astype(BF16)
    pos = lax.broadcasted_iota(jnp.int32, (sb, tm, FFN_COLS), 1).reshape(rows, FFN_COLS)

    def per_row(v):
        return jnp.broadcast_to(v, (sb, tm, FFN_COLS)).reshape(rows, FFN_COLS)

    acc = x
    for c in range(FFN_SPLIT):
        cs = slice(c * FFN_COLS, (c + 1) * FFN_COLS)
        a = _dot(hb, wup_ref[:, c * FFN_COLS:(c + 1) * FFN_COLS])
        val = _dot(hb, wup_ref[:, FFN_DIM + c * FFN_COLS:FFN_DIM + (c + 1) * FFN_COLS])
        prev2 = per_row(carry_ref[:, 0:1, cs])
        prev1 = per_row(carry_ref[:, 1:2, cs])
        a1 = jnp.where(pos == 0, prev1, pltpu.roll(a, 1, 0))
        a2 = jnp.where(pos == 0, prev2, jnp.where(pos == 1, prev1, pltpu.roll(a, 2, 0)))
        conv = cb_ref[:, cs] + cw_ref[0:1, cs] * a2 + cw_ref[1:2, cs] * a1 + cw_ref[2:3, cs] * a
        gated = (_gelu_tanh(conv) * val).astype(BF16)
        acc = acc + _dot(gated, wdn_ref[c * FFN_COLS:(c + 1) * FFN_COLS, :])
        carry_ref[:, 0:CONV_W - 1, cs] = a.reshape(sb, tm, FFN_COLS)[:, tm - (CONV_W - 1):tm, :]
    if final:
        acc = _rms(acc, nfin_ref[...])
    y_ref[...] = acc.reshape(sb, tm, D_MODEL)

    @pl.when(t == pl.num_programs(1) - 1)
    def _():
        cnew_ref[0] = carry_ref[:, 0:CONV_W - 1, :]


def _ffn(x, conv_state, nf, wup, cw, cb, wdn, nfin, c_buf, layer, final, sb, tm):
    B, L, _ = x.shape
    full = _resident
    return pl.pallas_call(
        functools.partial(_ffn_kernel, final=final, sb=sb, tm=tm),
        out_shape=(jax.ShapeDtypeStruct(x.shape, F32), jax.ShapeDtypeStruct(c_buf.shape, F32)),
        grid=(B // sb, L // tm),
        in_specs=[
            pl.BlockSpec((sb, tm, D_MODEL), lambda b, t: (b, t, 0)),
            pl.BlockSpec((sb, CONV_W - 1, FFN_DIM), lambda b, t: (b, 0, 0)),
            full(nf.shape), full(wup.shape), full(cw.shape), full(cb.shape), full(wdn.shape), full(nfin.shape),
            _aliased(),
        ],
        out_specs=(pl.BlockSpec((sb, tm, D_MODEL), lambda b, t: (b, t, 0)),
                   pl.BlockSpec((1, sb, CONV_W - 1, FFN_DIM), lambda b, t: (layer, b, 0, 0))),
        scratch_shapes=[pltpu.VMEM((sb, SUBLANES, FFN_DIM), F32)],
        input_output_aliases={8: 1},
        compiler_params=_params("parallel", "arbitrary"), name="ffn",
    )(x, conv_state, nf, wup, cw, cb, wdn, nfin, c_buf)


def _rope_tables(pos):
    half = MLA_ROPE // 2
    inv = ROPE_BASE ** (-jnp.arange(half, dtype=F32) / half)
    ang = pos.astype(F32)[:, None] * inv[None, :]
    cos, sin = jnp.cos(ang), jnp.sin(ang)
    return jnp.tile(cos, (1, 4)), jnp.concatenate([-sin, sin, -sin, sin], axis=1)


def _prep_layer(w_in_l, w_uq_l, w_ukv_l):
    kr = w_in_l[:, OFF_KR:OFF_KR + MLA_ROPE]
    win = jnp.concatenate([w_in_l[:, :OFF_KR], kr, kr, w_in_l[:, OFF_KR + MLA_ROPE:]], axis=1).astype(BF16)
    uq = w_uq_l.reshape(MLA_Q_RANK, MLA_HEADS, MLA_QK)
    wuq = jnp.concatenate([uq[:, :, :MLA_NOPE].reshape(MLA_Q_RANK, -1),
                           uq[:, :, MLA_NOPE:].reshape(MLA_Q_RANK, -1)], axis=1).astype(BF16)
    ukv = w_ukv_l.reshape(MLA_KV_RANK, MLA_HEADS, MLA_NOPE + MLA_V)
    wuk = ukv[:, :, :MLA_NOPE].reshape(MLA_KV_RANK, -1).astype(BF16)
    wuv = ukv[:, :, MLA_NOPE:].reshape(MLA_KV_RANK, -1).astype(BF16)
    return win, wuq, wuk, wuv


def _tiling(B, L):
    if L % ROW_TILE == 0:
        return 1, ROW_TILE
    sb = 1
    if L % SUBLANES == 0:
        for cand in range(1, B + 1):
            if B % cand == 0 and cand * L <= ROW_TILE:
                sb = cand
    return sb, L


def _state_buffers(depth, B, L):
    return (jnp.zeros((depth, B, L, MLA_KV_RANK), F32), jnp.zeros((depth, B, L, MLA_ROPE), F32),
            jnp.zeros((depth, B, HG_HEADS, HG_DK, HG_DV), F32), jnp.zeros((depth, B, CONV_W - 1, FFN_DIM), F32))


def kernel(x_prompt, x_sample, cache_mla_latent, cache_mla_krope, state_hgrn, state_ffn_conv, norm_mix, w_in, q_norm, w_uq, kv_norm, w_ukv, lb_logits, hgrn_norm, w_proj_a, w_proj_b, w_out, norm_ffn, w_up, conv_w, conv_b, w_down, norm_final):
    depth = w_in.shape[0]
    B, Lp, _ = x_prompt.shape
    Bs, Ls, _ = x_sample.shape
    past = cache_mla_latent.shape[2]

    sb_p, tm_p = _tiling(B, Lp)
    sb_s, tm_s = _tiling(Bs, Ls)
    cos_p, sin_p = _rope_tables(jnp.arange(Lp))
    cos_s, sin_s = _rope_tables(past + jnp.arange(Ls))
    cos_s, sin_s = jnp.tile(cos_s, (sb_s, 1)), jnp.tile(sin_s, (sb_s, 1))
    s0_p = jnp.zeros((B, HG_HEADS, HG_DK, HG_DV), F32)
    conv0_p = jnp.zeros((B, CONV_W - 1, FFN_DIM), F32)
    nfin = norm_final.reshape(1, D_MODEL)
    lbl = lb_logits.astype(F32)

    tk = ATTN_TK if Lp % ATTN_TK == 0 else Lp

    lat_p, kpe_p, hs_p, cv_p = _state_buffers(depth, B, Lp)
    lat_s, kpe_s, hs_s, cv_s = _state_buffers(depth, Bs, Ls)

    xp, xs = x_prompt, x_sample
    for l in range(depth):
        win, wuq, wuk, wuv = _prep_layer(w_in[l], w_uq[l], w_ukv[l])
        wuvt = wuv.T
        nmix = norm_mix[l].reshape(1, -1)
        qn = q_norm[l].reshape(1, -1)
        kvn = kv_norm[l].reshape(1, -1)
        gn = hgrn_norm[l].reshape(1, -1)
        nf = norm_ffn[l].reshape(1, -1)
        wpa, wpb, wo = w_proj_a[l].astype(BF16), w_proj_b[l].astype(BF16), w_out[l].astype(BF16)
        wup, wdn = w_up[l].astype(BF16), w_down[l].astype(BF16)
        cw, cb = conv_w[l], conv_b[l].reshape(1, -1)
        final = l == depth - 1

        q, k, vt, lat_p, kpe_p, h4, g = _inproj(xp, cos_p, sin_p, nmix, win, qn, wuq, kvn, wuk, wuvt, lbl,
                                                lat_p, kpe_p, l, sb_p, tm_p, True)
        oa = _attn_prompt(q, k, vt, tk)
        ob, hs_p = _hgrn(h4, gn, s0_p, hs_p, l, 512)
        x1 = _merge(xp, oa, ob, g, wpa, wpb, wo, sb_p, tm_p)
        xp, cv_p = _ffn(x1, conv0_p, nf, wup, cw, cb, wdn, nfin, cv_p, l, final, sb_p, tm_p)

        q, k, lat_s, kpe_s, h4, g = _inproj(xs, cos_s, sin_s, nmix, win, qn, wuq, kvn, wuk, wuvt, lbl,
                                            lat_s, kpe_s, l, sb_s, tm_s, False)
        oa = _attn_sample(q, k, lat_s, l, cache_mla_latent[l], cache_mla_krope[l], wuk, wuv)
        ob, hs_s = _hgrn(h4, gn, state_hgrn[l], hs_s, l, 512)
        x1 = _merge(xs, oa, ob, g, wpa, wpb, wo, sb_s, tm_s)
        xs, cv_s = _ffn(x1, state_ffn_conv[l], nf, wup, cw, cb, wdn, nfin, cv_s, l, final, sb_s, tm_s)

    return (xp, xs, lat_p, kpe_p, hs_p, cv_p, lat_s, kpe_s, hs_s, cv_s)
```

```python
import functools
import math

import numpy as np
import jax
import jax.numpy as jnp
from jax import lax
from jax.experimental import pallas as pl
from jax.experimental.pallas import tpu as pltpu

D_MODEL = 1024
CHUNK = 64
EPS = 1e-6
MLA_HEADS = 8
MLA_Q_RANK = 384
MLA_KV_RANK = 256
MLA_NOPE = 128
MLA_ROPE = 64
MLA_V = 128
ROPE_BASE = 10000.0
HG_HEADS = 4
HG_DK = 128
HG_DV = 128
FFN_DIM = 2816
CONV_W = 3

MLA_QK = MLA_NOPE + MLA_ROPE
HG_W = HG_HEADS * HG_DK
QK_PAD = 2 * MLA_NOPE
VT_ROWS = MLA_V + 16
LANES = 128
SUBLANES = 8
VMEM_LIMIT = 56 * 1024 * 1024
ROW_TILE = 512

OFF_CQ = 0
OFF_CKV = OFF_CQ + MLA_Q_RANK
OFF_KR = OFF_CKV + MLA_KV_RANK
OFF_HG = OFF_KR + 2 * MLA_ROPE
OFF_GATE = OFF_HG + 4 * HG_W
IN_COLS_R = OFF_GATE + 2 * D_MODEL

BF16 = jnp.bfloat16
F32 = jnp.float32
NT_DIMS = (((1,), (1,)), ((), ()))
TN_DIMS = (((0,), (0,)), ((), ()))


def _dot(a, b):
    return jnp.dot(a, b, preferred_element_type=F32)


def _dot_nt(a, b):
    return lax.dot_general(a, b, NT_DIMS, preferred_element_type=F32)


def _rms(x, g):
    return x * lax.rsqrt(jnp.mean(x * x, axis=-1, keepdims=True) + EPS) * g


def _sigmoid(x):
    return 0.5 * (jnp.tanh(0.5 * x) + 1.0)


def _resident(shape):
    return pl.BlockSpec(shape, lambda *_: (0,) * len(shape), pipeline_mode=pl.Buffered(1))


def _aliased():
    return pl.BlockSpec(memory_space=pl.ANY)


def _params(*sem):
    return pltpu.CompilerParams(dimension_semantics=sem, vmem_limit_bytes=VMEM_LIMIT)


def _inproj_kernel(x_ref, cos_ref, sin_ref, nmix_ref, win_ref, qn_ref, wuq_ref, kvn_ref, wuk_ref, wuvt_ref, lbl_ref,
                   lat_in, kpe_in, *out_refs, qscale, layer, sb, tm, with_vt):
    del lat_in, kpe_in
    if with_vt:
        q_ref, k_ref, vt_ref, lat_ref, kpe_ref, h4_ref, g_ref = out_refs
    else:
        q_ref, k_ref, lat_ref, kpe_ref, h4_ref, g_ref = out_refs
    rows = sb * tm
    x = x_ref[...].reshape(rows, D_MODEL)
    hb = _rms(x, nmix_ref[...]).astype(BF16)
    cosf = cos_ref[...]
    sinf = sin_ref[...]
    lane = lax.broadcasted_iota(jnp.int32, cosf.shape, 1)
    first_half = (lane % MLA_ROPE) < (MLA_ROPE // 2)
    low_head = lane < MLA_ROPE

    def rot(v):
        sw = jnp.where(first_half, pltpu.roll(v, LANES - MLA_ROPE // 2, 1), pltpu.roll(v, MLA_ROPE // 2, 1))
        return v * cosf + sw * sinf

    cq = _dot(hb, win_ref[:, OFF_CQ:OFF_CKV])
    ckv = _dot(hb, win_ref[:, OFF_CKV:OFF_KR])
    kr = _dot(hb, win_ref[:, OFF_KR:OFF_HG])
    h4 = _dot(hb, win_ref[:, OFF_HG:OFF_GATE])
    cqn = _rms(cq, qn_ref[...]).astype(BF16)
    lat = _rms(ckv, kvn_ref[...])
    lat_ref[0] = lat.reshape(sb, tm, MLA_KV_RANK)
    latb = lat.astype(BF16)
    q = _dot(cqn, wuq_ref[...]) * qscale
    kn = _dot(latb, wuk_ref[...])
    if with_vt:
        vt = _dot_nt(wuvt_ref[...], latb).astype(BF16)
        ones = jnp.ones((VT_ROWS - MLA_V, rows), BF16)
        for h in range(MLA_HEADS):
            vt_ref[0, h * VT_ROWS:h * VT_ROWS + MLA_V, :] = vt[h * MLA_V:(h + 1) * MLA_V]
            vt_ref[0, h * VT_ROWS + MLA_V:(h + 1) * VT_ROWS, :] = ones
    graw = _dot(hb, win_ref[:, OFF_GATE:IN_COLS_R])

    kpe = rot(kr)
    kpe_ref[0] = kpe[:, :MLA_ROPE].reshape(sb, tm, MLA_ROPE)
    kpeb = kpe.astype(BF16)

    nope_w = MLA_HEADS * MLA_NOPE
    for g in range(MLA_HEADS // 2):
        r = rot(q[:, nope_w + g * LANES: nope_w + (g + 1) * LANES])
        for hh in range(2):
            h = 2 * g + hh
            rh = jnp.where(low_head if hh == 0 else jnp.logical_not(low_head), r, 0.0)
            qh = jnp.concatenate([q[:, h * MLA_NOPE:(h + 1) * MLA_NOPE], rh], axis=1).astype(BF16)
            kh = jnp.concatenate([kn[:, h * MLA_NOPE:(h + 1) * MLA_NOPE].astype(BF16), kpeb], axis=1)
            q_ref[:, h] = qh.reshape(sb, tm, QK_PAD)
            k_ref[:, h] = kh.reshape(sb, tm, QK_PAD)

    lg = lbl_ref[...]
    e = jnp.exp(lg - jnp.max(lg, axis=0, keepdims=True))
    sm = e / jnp.sum(e, axis=0, keepdims=True)
    lb = sm[0:1] * 0.0
    for i in range(1, layer + 1):
        lb = lb + sm[i:i + 1]
    hq, hf, hi, hg = (h4[:, k * HG_W:(k + 1) * HG_W] for k in range(4))
    h4 = jnp.concatenate([hq * _sigmoid(hq) * HG_DK ** -0.5, lb + (1.0 - lb) * _sigmoid(hf), hi, hg * _sigmoid(hg)],
                         axis=1)
    h4_ref[...] = h4.reshape(sb, tm, 4 * HG_W)
    gates = _sigmoid(graw)
    g_ref[...] = gates.astype(BF16).reshape(sb, tm, 2 * D_MODEL)


def _inproj(x, cosf, sinf, nmix, win, qn, wuq, kvn, wuk, wuvt, lbl, lat_buf, kpe_buf, layer, sb, tm, with_vt):
    B, L, _ = x.shape
    H = MLA_HEADS
    qscale = (1.0 / math.sqrt(MLA_QK)) * math.log2(math.e)
    full = _resident
    seq = lambda w: pl.BlockSpec((sb, tm, w), lambda b, t: (b, t, 0))
    stacked = lambda w: pl.BlockSpec((1, sb, tm, w), lambda b, t: (layer, b, t, 0))
    qk_shape = jax.ShapeDtypeStruct((B, H, L, QK_PAD), BF16)
    qk_spec = pl.BlockSpec((sb, H, tm, QK_PAD), lambda b, t: (b, 0, t, 0))
    out_shape = [qk_shape, qk_shape]
    out_specs = [qk_spec, qk_spec]
    if with_vt:
        assert sb == 1
        out_shape.append(jax.ShapeDtypeStruct((B, H * VT_ROWS, L), BF16))
        out_specs.append(pl.BlockSpec((1, H * VT_ROWS, tm), lambda b, t: (b, 0, t)))
    lat_idx = len(out_shape)
    out_shape += [jax.ShapeDtypeStruct(lat_buf.shape, F32), jax.ShapeDtypeStruct(kpe_buf.shape, F32),
                  jax.ShapeDtypeStruct((B, L, 4 * HG_W), F32), jax.ShapeDtypeStruct((B, L, 2 * D_MODEL), BF16)]
    out_specs += [stacked(MLA_KV_RANK), stacked(MLA_ROPE), seq(4 * HG_W), seq(2 * D_MODEL)]
    in_specs = [
        seq(D_MODEL),
        pl.BlockSpec((sb * tm, LANES), lambda b, t: (t, 0)),
        pl.BlockSpec((sb * tm, LANES), lambda b, t: (t, 0)),
        full(nmix.shape), full(win.shape), full(qn.shape), full(wuq.shape), full(kvn.shape),
        full(wuk.shape), full(wuvt.shape), full(lbl.shape),
        _aliased(), _aliased(),
    ]
    return pl.pallas_call(
        functools.partial(_inproj_kernel, qscale=qscale, layer=layer, sb=sb, tm=tm, with_vt=with_vt),
        out_shape=tuple(out_shape), grid=(B // sb, L // tm), in_specs=in_specs, out_specs=tuple(out_specs),
        input_output_aliases={11: lat_idx, 12: lat_idx + 1},
        compiler_params=_params("parallel", "arbitrary"), name="inproj",
    )(x, cosf, sinf, nmix, win, qn, wuq, kvn, wuk, wuvt, lbl, lat_buf, kpe_buf)


ATTN_TK = 512
ATTN_SHIFT_MAX = 64.0


def _attn_kernel(q_ref, k_ref, vt_ref, o_ref, acc_ref, *, seq, tk):
    nst = seq // tk
    kc = lax.broadcasted_iota(jnp.int32, (tk, tk), 0) // CHUNK
    qc = lax.broadcasted_iota(jnp.int32, (tk, tk), 1) // CHUNK
    visible = kc <= qc

    def finish(acc):
        return (acc[:MLA_V] / acc[MLA_V:MLA_V + 1]).T.astype(BF16)

    def scores(j):
        q0 = j * tk
        s = _dot_nt(k_ref[0, 0, q0:q0 + tk, :], q_ref[0, 0, q0:, :])
        sm = jnp.where(visible, s[:, :tk], -jnp.inf)
        return sm if s.shape[1] == tk else jnp.concatenate([sm, s[:, tk:]], axis=1)

    s = scores(0)
    m = jnp.max(s, axis=0, keepdims=True)
    acc_ref[...] = _dot(vt_ref[0, :, 0:tk], jnp.exp2(s - m).astype(BF16))
    excess = jnp.zeros((1, seq), F32)
    for j in range(1, nst):
        q0 = j * tk
        s = scores(j)
        mcur = m[:, q0:]
        p = jnp.exp2(s - mcur).astype(BF16)
        mx = jnp.max(s, axis=0, keepdims=True)
        m_new = jnp.maximum(mcur, mx)
        acc_ref[:, q0:] = jnp.exp2(mcur - m_new) * (acc_ref[:, q0:] + _dot(vt_ref[0, :, q0:q0 + tk], p))
        excess = jnp.concatenate([excess[:, :q0], jnp.maximum(excess[:, q0:], mx - mcur)], axis=1)
        m = jnp.concatenate([m[:, :q0], m_new], axis=1)
    for cb in range(nst):
        o_ref[0, cb * tk:(cb + 1) * tk, :] = finish(acc_ref[:, cb * tk:(cb + 1) * tk])

    @pl.when(jnp.max(excess) > ATTN_SHIFT_MAX)
    def _():
        def column_block(cb, carry):
            c0 = pl.multiple_of(cb * tk, tk)
            qb = q_ref[0, 0, pl.ds(c0, tk), :]
            qcol = (c0 + lax.broadcasted_iota(jnp.int32, (tk, tk), 1)) // CHUNK

            def key_block(j, mla):
                mo, acc = mla
                k0 = pl.multiple_of(j * tk, tk)
                sc = _dot_nt(k_ref[0, 0, pl.ds(k0, tk), :], qb)
                krow = (k0 + lax.broadcasted_iota(jnp.int32, (tk, tk), 0)) // CHUNK
                sc = jnp.where(krow <= qcol, sc, -jnp.inf)
                mn = jnp.maximum(mo, jnp.max(sc, axis=0, keepdims=True))
                pv = _dot(vt_ref[0, :, pl.ds(k0, tk)], jnp.exp2(sc - mn).astype(BF16))
                return mn, jnp.exp2(mo - mn) * acc + pv

            init = (jnp.full((1, tk), -jnp.inf, F32), jnp.zeros((VT_ROWS, tk), F32))
            _, acc = lax.fori_loop(0, cb + 1, key_block, init)
            o_ref[0, pl.ds(c0, tk), :] = finish(acc)
            return carry
        lax.fori_loop(0, nst, column_block, 0)


def _attn_prompt(q, k, vt, tk):
    B, H, L, _ = q.shape
    return pl.pallas_call(
        functools.partial(_attn_kernel, seq=L, tk=tk),
        out_shape=jax.ShapeDtypeStruct((B, L, H * MLA_V), BF16),
        grid=(B, H),
        in_specs=[
            pl.BlockSpec((1, 1, L, QK_PAD), lambda b, h: (b, h, 0, 0)),
            pl.BlockSpec((1, 1, L, QK_PAD), lambda b, h: (b, h, 0, 0)),
            pl.BlockSpec((1, VT_ROWS, L), lambda b, h: (b, h, 0)),
        ],
        out_specs=pl.BlockSpec((1, L, MLA_V), lambda b, h: (b, 0, h)),
        scratch_shapes=[pltpu.VMEM((VT_ROWS, L), F32)],
        compiler_params=_params("parallel", "parallel"), name="attn_prompt",
    )(q, k, vt)


def _attn_sample_kernel(q_ref, kn_ref, latn_ref, latp_ref, kpep_ref, wuk_ref, wuv_ref, o_ref):
    latp = latp_ref[0].astype(BF16)
    latn = latn_ref[0, 0].astype(BF16)
    knp = _dot(latp, wuk_ref[...]).astype(BF16)
    vp = _dot(latp, wuv_ref[...]).astype(BF16)
    vn = _dot(latn, wuv_ref[...]).astype(BF16)
    kp = kpep_ref[0].astype(BF16)
    kpd = jnp.concatenate([kp, kp], axis=1)
    for h in range(MLA_HEADS):
        sl = slice(h * MLA_NOPE, (h + 1) * MLA_NOPE)
        qh = q_ref[0, h]
        khp = jnp.concatenate([knp[:, sl], kpd], axis=1)
        s1 = _dot_nt(qh, khp)
        s2 = _dot_nt(qh, kn_ref[0, h])
        m = jnp.maximum(jnp.max(s1, axis=1, keepdims=True), jnp.max(s2, axis=1, keepdims=True))
        p1 = jnp.exp2(s1 - m)
        p2 = jnp.exp2(s2 - m)
        l = jnp.sum(p1, axis=1, keepdims=True) + jnp.sum(p2, axis=1, keepdims=True)
        o = _dot(p1.astype(BF16), vp[:, sl]) + _dot(p2.astype(BF16), vn[:, sl])
        o_ref[0, :, sl] = (o / l).astype(BF16)


def _attn_sample(q, kn, lat_buf, layer, latp, kpep, wuk, wuv):
    B, H, Ls, _ = q.shape
    P = latp.shape[1]
    full = _resident
    return pl.pallas_call(
        _attn_sample_kernel,
        out_shape=jax.ShapeDtypeStruct((B, Ls, H * MLA_V), BF16),
        grid=(B,),
        in_specs=[
            pl.BlockSpec((1, H, Ls, QK_PAD), lambda b: (b, 0, 0, 0)),
            pl.BlockSpec((1, H, Ls, QK_PAD), lambda b: (b, 0, 0, 0)),
            pl.BlockSpec((1, 1, Ls, MLA_KV_RANK), lambda b: (layer, b, 0, 0)),
            pl.BlockSpec((1, P, MLA_KV_RANK), lambda b: (b, 0, 0)),
            pl.BlockSpec((1, P, MLA_ROPE), lambda b: (b, 0, 0)),
            full(wuk.shape), full(wuv.shape),
        ],
        out_specs=pl.BlockSpec((1, Ls, H * MLA_V), lambda b: (b, 0, 0)),
        compiler_params=_params("parallel"), name="attn_sample",
    )(q, kn, lat_buf, latp, kpep, wuk, wuv)


DIAG_BLOCK = SUBLANES
HGRN_UNROLL = 8


def _hgrn_kernel(h4_ref, gn_ref, s0_ref, sbuf_in, o_ref, sout_ref, st_ref, pb_ref, ob_ref, lvl_ref,
                 *, c, nchunk, unroll):
    del sbuf_in
    t = pl.program_id(1)

    @pl.when(t == 0)
    def _():
        for h in range(HG_HEADS):
            st_ref[h] = s0_ref[0, h].T
        rt4 = lax.broadcasted_iota(jnp.int32, (HG_HEADS * c, HG_HEADS * c), 0)
        ct4 = lax.broadcasted_iota(jnp.int32, (HG_HEADS * c, HG_HEADS * c), 1)
        lvl = jnp.zeros((HG_HEADS * c, HG_HEADS * c), jnp.int32)
        n, level = c // 2, 0
        while n >= DIAG_BLOCK:
            level += 1
            hit = ((rt4 // (2 * n)) == (ct4 // (2 * n))) & ((rt4 % (2 * n)) >= n) & ((ct4 % (2 * n)) < n)
            lvl = jnp.where(hit, level, lvl)
            n //= 2
        lvl_ref[...] = lvl

    gn = gn_ref[...]

    rt = lax.broadcasted_iota(jnp.int32, (c, c), 0)
    ct = lax.broadcasted_iota(jnp.int32, (c, c), 1)
    tril = jnp.where(ct <= rt, 1.0, 0.0).astype(BF16)
    tril3 = jnp.concatenate([tril, tril, tril], axis=1)
    nblk = c // DIAG_BLOCK

    def chunk(ci, u):
        r0 = pl.multiple_of(ci * c, c)

        def seg(k):
            return h4_ref[0, pl.ds(r0, c), pl.ds(k * HG_W, HG_W)]

        def stack(w):
            return jnp.concatenate([w[:, h * HG_DK:(h + 1) * HG_DK] for h in range(HG_HEADS)], axis=0)

        qq_w, f_w, hi_w, gs_w = seg(0), seg(1), seg(2), seg(3)
        logf = jnp.log(f_w)
        kk_w = 1.0 - f_w
        vb_w = hi_w.astype(BF16)

        p0 = logf.astype(BF16)
        r1 = logf - p0.astype(F32)
        p1 = r1.astype(BF16)
        p2 = (r1 - p1.astype(F32)).astype(BF16)
        b_w = _dot(tril3, jnp.concatenate([p0, p1, p2], axis=0))
        bend_w = b_w[c - 1:c, :]
        ebend_w = jnp.exp(bend_w)

        inter = _dot_nt(stack((qq_w * jnp.exp(b_w)).astype(BF16)),
                        st_ref[...].reshape(HG_HEADS * HG_DV, HG_DK).astype(BF16))
        kd_w = (kk_w * jnp.exp(bend_w - b_w)).astype(BF16)
        upd = lax.dot_general(vb_w, kd_w, TN_DIMS, preferred_element_type=F32)
        for h in range(HG_HEADS):
            hs = slice(h * HG_DK, (h + 1) * HG_DK)
            st_ref[h] = st_ref[h] * ebend_w[:, hs] + upd[hs, hs]

        a_all = jnp.zeros((HG_HEADS * c, HG_HEADS * c), F32)
        n = c // 2
        level = 0
        while n >= DIAG_BLOCK:
            qparts, kparts = [], []
            for j in range(c // (2 * n)):
                lo, mid, hi_ = 2 * n * j, 2 * n * j + n, 2 * n * (j + 1)
                ref = b_w[mid - 1:mid, :]
                qparts += [jnp.zeros((n, HG_W), F32), qq_w[mid:hi_] * jnp.exp(b_w[mid:hi_] - ref)]
                kparts += [kk_w[lo:mid] * jnp.exp(ref - b_w[lo:mid]), jnp.zeros((n, HG_W), F32)]
            an = _dot_nt(stack(jnp.concatenate(qparts, axis=0)).astype(BF16),
                         stack(jnp.concatenate(kparts, axis=0)).astype(BF16))
            level += 1
            a_all = jnp.where(lvl_ref[...] == level, an, a_all)
            n //= 2
        intra = _dot(a_all.astype(BF16), stack(vb_w))

        for h in range(HG_HEADS):
            sl = u * HG_HEADS + h
            hs = slice(h * HG_DK, (h + 1) * HG_DK)
            qq, kk, b, hi = qq_w[:, hs], kk_w[:, hs], b_w[:, hs], hi_w[:, hs]
            o = inter[h * c:(h + 1) * c, hs] + intra[h * c:(h + 1) * c]

            pb_ref[sl, 0] = qq
            pb_ref[sl, 1] = kk
            pb_ref[sl, 2] = b
            pb_ref[sl, 3] = hi
            rows = [[pb_ref[sl, k, pl.ds(a, nblk, stride=DIAG_BLOCK), :] for a in range(DIAG_BLOCK)] for k in range(4)]
            for a in range(DIAG_BLOCK):
                oa = jnp.zeros((nblk, HG_DV), F32)
                for e_ in range(a + 1):
                    w = rows[0][a] * rows[1][e_]
                    if e_ < a:
                        w = w * jnp.exp(rows[2][a] - rows[2][e_])
                    oa = oa + jnp.sum(w, axis=1, keepdims=True) * rows[3][e_]
                ob_ref[sl, pl.ds(a, nblk, stride=DIAG_BLOCK), :] = oa
            o = o + ob_ref[sl]

            ob = _rms(o, gn) * gs_w[:, hs]
            o_ref[0, pl.ds(r0, c), pl.ds(h * HG_DV, HG_DV)] = ob.astype(BF16)

    def unrolled(cu, carry):
        for u in range(unroll):
            chunk(cu * unroll + u, u)
        return carry

    lax.fori_loop(0, nchunk // unroll, unrolled, 0)

    @pl.when(t == pl.num_programs(1) - 1)
    def _():
        for h in range(HG_HEADS):
            sout_ref[0, 0, h] = st_ref[h].T


def _hgrn(h4, gnorm, s0, s_buf, layer, tt):
    B, L, _ = h4.shape
    c = min(CHUNK, L)
    tt = min(tt, L)
    nchunk = tt // c
    unroll = HGRN_UNROLL if nchunk % HGRN_UNROLL == 0 else 1
    full = _resident
    return pl.pallas_call(
        functools.partial(_hgrn_kernel, c=c, nchunk=nchunk, unroll=unroll),
        out_shape=(jax.ShapeDtypeStruct((B, L, HG_HEADS * HG_DV), BF16),
                   jax.ShapeDtypeStruct(s_buf.shape, F32)),
        grid=(B, L // tt),
        in_specs=[
            pl.BlockSpec((1, tt, 4 * HG_W), lambda b, t: (b, t, 0)),
            full(gnorm.shape),
            pl.BlockSpec((1, HG_HEADS, HG_DK, HG_DV), lambda b, t: (b, 0, 0, 0)),
            _aliased(),
        ],
        out_specs=(pl.BlockSpec((1, tt, HG_HEADS * HG_DV), lambda b, t: (b, t, 0)),
                   pl.BlockSpec((1, 1, HG_HEADS, HG_DK, HG_DV), lambda b, t: (layer, b, 0, 0, 0))),
        scratch_shapes=[pltpu.VMEM((HG_HEADS, HG_DV, HG_DK), F32),
                        pltpu.VMEM((unroll * HG_HEADS, 4, c, HG_DK), F32),
                        pltpu.VMEM((unroll * HG_HEADS, c, HG_DV), F32),
                        pltpu.VMEM((HG_HEADS * c, HG_HEADS * c), jnp.int32)],
        input_output_aliases={3: 1},
        compiler_params=_params("parallel", "arbitrary"), name="hgrn",
    )(h4, gnorm, s0, s_buf)


def _merge_kernel(x_ref, oa_ref, ob_ref, g_ref, wpa_ref, wpb_ref, wout_ref, y_ref, *, sb, tm):
    rows = sb * tm
    g = g_ref[...].reshape(rows, 2 * D_MODEL).astype(F32)
    oa = oa_ref[...].reshape(rows, MLA_HEADS * MLA_V)
    ob = ob_ref[...].reshape(rows, HG_HEADS * HG_DV)
    mix = g[:, :D_MODEL] * _dot(oa, wpa_ref[...]) + g[:, D_MODEL:] * _dot(ob, wpb_ref[...])
    y = x_ref[...].reshape(rows, D_MODEL) + _dot(mix.astype(BF16), wout_ref[...])
    y_ref[...] = y.reshape(sb, tm, D_MODEL)


def _merge(x, oa, ob, g, wpa, wpb, wout, sb, tm):
    B, L, _ = x.shape
    full = _resident
    row = lambda w: pl.BlockSpec((sb, tm, w), lambda b, t: (b, t, 0))
    return pl.pallas_call(
        functools.partial(_merge_kernel, sb=sb, tm=tm),
        out_shape=jax.ShapeDtypeStruct(x.shape, F32),
        grid=(B // sb, L // tm),
        in_specs=[row(D_MODEL), row(oa.shape[2]), row(ob.shape[2]), row(2 * D_MODEL),
                  full(wpa.shape), full(wpb.shape), full(wout.shape)],
        out_specs=row(D_MODEL),
        compiler_params=_params("parallel", "parallel"), name="merge",
    )(x, oa, ob, g, wpa, wpb, wout)


FFN_SPLIT = 2
FFN_COLS = FFN_DIM // FFN_SPLIT


def _gelu_tanh(x):
    return 0.5 * x * (1.0 + jnp.tanh(math.sqrt(2.0 / math.pi) * (x + 0.044715 * (x * x * x))))


def _ffn_kernel(x_ref, cs_ref, nf_ref, wup_ref, cw_ref, cb_ref, wdn_ref, nfin_ref, cbuf_in, y_ref, cnew_ref,
                carry_ref, *, final, sb, tm):
    del cbuf_in
    t = pl.program_id(1)
    rows = sb * tm

    @pl.when(t == 0)
    def _():
        carry_ref[:, 0:CONV_W - 1, :] = cs_ref[...]

    x = x_ref[...].reshape(rows, D_MODEL)
    hb = _rms(x, nf_ref[...]).astype(BF16)
    pos = lax.broadcasted_iota(jnp.int32, (sb, tm, FFN_COLS), 1).reshape(rows, FFN_COLS)

    def per_row(v):
        return jnp.broadcast_to(v, (sb, tm, FFN_COLS)).reshape(rows, FFN_COLS)

    acc = x
    for c in range(FFN_SPLIT):
        cs = slice(c * FFN_COLS, (c + 1) * FFN_COLS)
        a = _dot(hb, wup_ref[:, c * FFN_COLS:(c + 1) * FFN_COLS])
        val = _dot(hb, wup_ref[:, FFN_DIM + c * FFN_COLS:FFN_DIM + (c + 1) * FFN_COLS])
        prev2 = per_row(carry_ref[:, 0:1, cs])
        prev1 = per_row(carry_ref[:, 1:2, cs])
        a1 = jnp.where(pos == 0, prev1, pltpu.roll(a, 1, 0))
        a2 = jnp.where(pos == 0, prev2, jnp.where(pos == 1, prev1, pltpu.roll(a, 2, 0)))
        conv = cb_ref[:, cs] + cw_ref[0:1, cs] * a2 + cw_ref[1:2, cs] * a1 + cw_ref[2:3, cs] * a
        gated = (_gelu_tanh(conv) * val).astype(BF16)
        acc = acc + _dot(gated, wdn_ref[c * FFN_COLS:(c + 1) * FFN_COLS, :])
        carry_ref[:, 0:CONV_W - 1, cs] = a.reshape(sb, tm, FFN_COLS)[:, tm - (CONV_W - 1):tm, :]
    if final:
        acc = _rms(acc, nfin_ref[...])
    y_ref[...] = acc.reshape(sb, tm, D_MODEL)

    @pl.when(t == pl.num_programs(1) - 1)
    def _():
        cnew_ref[0] = carry_ref[:, 0:CONV_W - 1, :]


def _ffn(x, conv_state, nf, wup, cw, cb, wdn, nfin, c_buf, layer, final, sb, tm):
    B, L, _ = x.shape
    full = _resident
    return pl.pallas_call(
        functools.partial(_ffn_kernel, final=final, sb=sb, tm=tm),
        out_shape=(jax.ShapeDtypeStruct(x.shape, F32), jax.ShapeDtypeStruct(c_buf.shape, F32)),
        grid=(B // sb, L // tm),
        in_specs=[
            pl.BlockSpec((sb, tm, D_MODEL), lambda b, t: (b, t, 0)),
            pl.BlockSpec((sb, CONV_W - 1, FFN_DIM), lambda b, t: (b, 0, 0)),
            full(nf.shape), full(wup.shape), full(cw.shape), full(cb.shape), full(wdn.shape), full(nfin.shape),
            _aliased(),
        ],
        out_specs=(pl.BlockSpec((sb, tm, D_MODEL), lambda b, t: (b, t, 0)),
                   pl.BlockSpec((1, sb, CONV_W - 1, FFN_DIM), lambda b, t: (layer, b, 0, 0))),
        scratch_shapes=[pltpu.VMEM((sb, SUBLANES, FFN_DIM), F32)],
        input_output_aliases={8: 1},
        compiler_params=_params("parallel", "arbitrary"), name="ffn",
    )(x, conv_state, nf, wup, cw, cb, wdn, nfin, c_buf)


def _rope_tables(pos):
    half = MLA_ROPE // 2
    inv = ROPE_BASE ** (-jnp.arange(half, dtype=F32) / half)
    ang = pos.astype(F32)[:, None] * inv[None, :]
    cos, sin = jnp.cos(ang), jnp.sin(ang)
    return jnp.tile(cos, (1, 4)), jnp.concatenate([-sin, sin, -sin, sin], axis=1)


def _prep_layer(w_in_l, w_uq_l, w_ukv_l):
    kr = w_in_l[:, OFF_KR:OFF_KR + MLA_ROPE]
    win = jnp.concatenate([w_in_l[:, :OFF_KR], kr, kr, w_in_l[:, OFF_KR + MLA_ROPE:]], axis=1).astype(BF16)
    uq = w_uq_l.reshape(MLA_Q_RANK, MLA_HEADS, MLA_QK)
    wuq = jnp.concatenate([uq[:, :, :MLA_NOPE].reshape(MLA_Q_RANK, -1),
                           uq[:, :, MLA_NOPE:].reshape(MLA_Q_RANK, -1)], axis=1).astype(BF16)
    ukv = w_ukv_l.reshape(MLA_KV_RANK, MLA_HEADS, MLA_NOPE + MLA_V)
    wuk = ukv[:, :, :MLA_NOPE].reshape(MLA_KV_RANK, -1).astype(BF16)
    wuv = ukv[:, :, MLA_NOPE:].reshape(MLA_KV_RANK, -1).astype(BF16)
    return win, wuq, wuk, wuv


def _tiling(B, L):
    if L % ROW_TILE == 0:
        return 1, ROW_TILE
    sb = 1
    if L % SUBLANES == 0:
        for cand in range(1, B + 1):
            if B % cand == 0 and cand * L <= ROW_TILE:
                sb = cand
    return sb, L


def _state_buffers(depth, B, L):
    return (jnp.zeros((depth, B, L, MLA_KV_RANK), F32), jnp.zeros((depth, B, L, MLA_ROPE), F32),
            jnp.zeros((depth, B, HG_HEADS, HG_DK, HG_DV), F32), jnp.zeros((depth, B, CONV_W - 1, FFN_DIM), F32))


def kernel(x_prompt, x_sample, cache_mla_latent, cache_mla_krope, state_hgrn, state_ffn_conv, norm_mix, w_in, q_norm, w_uq, kv_norm, w_ukv, lb_logits, hgrn_norm, w_proj_a, w_proj_b, w_out, norm_ffn, w_up, conv_w, conv_b, w_down, norm_final):
    depth = w_in.shape[0]
    B, Lp, _ = x_prompt.shape
    Bs, Ls, _ = x_sample.shape
    past = cache_mla_latent.shape[2]

    sb_p, tm_p = _tiling(B, Lp)
    sb_s, tm_s = _tiling(Bs, Ls)
    cos_p, sin_p = _rope_tables(jnp.arange(Lp))
    cos_s, sin_s = _rope_tables(past + jnp.arange(Ls))
    cos_s, sin_s = jnp.tile(cos_s, (sb_s, 1)), jnp.tile(sin_s, (sb_s, 1))
    s0_p = jnp.zeros((B, HG_HEADS, HG_DK, HG_DV), F32)
    conv0_p = jnp.zeros((B, CONV_W - 1, FFN_DIM), F32)
    nfin = norm_final.reshape(1, D_MODEL)
    lbl = lb_logits.astype(F32)

    tk = ATTN_TK if Lp % ATTN_TK == 0 else Lp

    lat_p, kpe_p, hs_p, cv_p = _state_buffers(depth, B, Lp)
    lat_s, kpe_s, hs_s, cv_s = _state_buffers(depth, Bs, Ls)

    xp, xs = x_prompt, x_sample
    for l in range(depth):
        win, wuq, wuk, wuv = _prep_layer(w_in[l], w_uq[l], w_ukv[l])
        wuvt = wuv.T
        nmix = norm_mix[l].reshape(1, -1)
        qn = q_norm[l].reshape(1, -1)
        kvn = kv_norm[l].reshape(1, -1)
        gn = hgrn_norm[l].reshape(1, -1)
        nf = norm_ffn[l].reshape(1, -1)
        wpa, wpb, wo = w_proj_a[l].astype(BF16), w_proj_b[l].astype(BF16), w_out[l].astype(BF16)
        wup, wdn = w_up[l].astype(BF16), w_down[l].astype(BF16)
        cw, cb = conv_w[l], conv_b[l].reshape(1, -1)
        final = l == depth - 1

        q, k, vt, lat_p, kpe_p, h4, g = _inproj(xp, cos_p, sin_p, nmix, win, qn, wuq, kvn, wuk, wuvt, lbl,
                                                lat_p, kpe_p, l, sb_p, tm_p, True)
        oa = _attn_prompt(q, k, vt, tk)
        ob, hs_p = _hgrn(h4, gn, s0_p, hs_p, l, 512)
        x1 = _merge(xp, oa, ob, g, wpa, wpb, wo, sb_p, tm_p)
        xp, cv_p = _ffn(x1, conv0_p, nf, wup, cw, cb, wdn, nfin, cv_p, l, final, sb_p, tm_p)

        q, k, lat_s, kpe_s, h4, g = _inproj(xs, cos_s, sin_s, nmix, win, qn, wuq, kvn, wuk, wuvt, lbl,
                                            lat_s, kpe_s, l, sb_s, tm_s, False)
        oa = _attn_sample(q, k, lat_s, l, cache_mla_latent[l], cache_mla_krope[l], wuk, wuv)
        ob, hs_s = _hgrn(h4, gn, state_hgrn[l], hs_s, l, 512)
        x1 = _merge(xs, oa, ob, g, wpa, wpb, wo, sb_s, tm_s)
        xs, cv_s = _ffn(x1, state_ffn_conv[l], nf, wup, cw, cb, wdn, nfin, cv_s, l, final, sb_s, tm_s)

    return (xp, xs, lat_p, kpe_p, hs_p, cv_p, lat_s, kpe_s, hs_s, cv_s)
```

```python
import functools
import math

import numpy as np
import jax
import jax.numpy as jnp
from jax import lax
from jax.experimental import pallas as pl
from jax.experimental.pallas import tpu as pltpu

D_MODEL = 1024
CHUNK = 64
EPS = 1e-6
MLA_HEADS = 8
MLA_Q_RANK = 384
MLA_KV_RANK = 256
MLA_NOPE = 128
MLA_ROPE = 64
MLA_V = 128
ROPE_BASE = 10000.0
HG_HEADS = 4
HG_DK = 128
HG_DV = 128
FFN_DIM = 2816
CONV_W = 3

MLA_QK = MLA_NOPE + MLA_ROPE
HG_W = HG_HEADS * HG_DK
QK_PAD = 2 * MLA_NOPE
VT_ROWS = MLA_V + 16
LANES = 128
SUBLANES = 8
VMEM_LIMIT = 56 * 1024 * 1024
ROW_TILE = 512

OFF_CQ = 0
OFF_CKV = OFF_CQ + MLA_Q_RANK
OFF_KR = OFF_CKV + MLA_KV_RANK
OFF_HG = OFF_KR + 2 * MLA_ROPE
OFF_GATE = OFF_HG + 4 * HG_W
IN_COLS_R = OFF_GATE + 2 * D_MODEL

BF16 = jnp.bfloat16
F32 = jnp.float32
NT_DIMS = (((1,), (1,)), ((), ()))
TN_DIMS = (((0,), (0,)), ((), ()))


def _dot(a, b):
    return jnp.dot(a, b, preferred_element_type=F32)


def _dot_nt(a, b):
    return lax.dot_general(a, b, NT_DIMS, preferred_element_type=F32)


def _rms(x, g):
    return x * lax.rsqrt(jnp.mean(x * x, axis=-1, keepdims=True) + EPS) * g


def _sigmoid(x):
    return 0.5 * (jnp.tanh(0.5 * x) + 1.0)


def _resident(shape):
    return pl.BlockSpec(shape, lambda *_: (0,) * len(shape), pipeline_mode=pl.Buffered(1))


def _aliased():
    return pl.BlockSpec(memory_space=pl.ANY)


def _params(*sem):
    return pltpu.CompilerParams(dimension_semantics=sem, vmem_limit_bytes=VMEM_LIMIT)


def _inproj_kernel(x_ref, cos_ref, sin_ref, nmix_ref, win_ref, qn_ref, wuq_ref, kvn_ref, wuk_ref, wuvt_ref, lbl_ref,
                   lat_in, kpe_in, *out_refs, qscale, layer, sb, tm, with_vt):
    del lat_in, kpe_in
    if with_vt:
        q_ref, k_ref, vt_ref, lat_ref, kpe_ref, h4_ref, g_ref = out_refs
    else:
        q_ref, k_ref, lat_ref, kpe_ref, h4_ref, g_ref = out_refs
    rows = sb * tm
    x = x_ref[...].reshape(rows, D_MODEL)
    hb = _rms(x, nmix_ref[...]).astype(BF16)
    cosf = cos_ref[...]
    sinf = sin_ref[...]
    lane = lax.broadcasted_iota(jnp.int32, cosf.shape, 1)
    first_half = (lane % MLA_ROPE) < (MLA_ROPE // 2)
    low_head = lane < MLA_ROPE

    def rot(v):
        sw = jnp.where(first_half, pltpu.roll(v, LANES - MLA_ROPE // 2, 1), pltpu.roll(v, MLA_ROPE // 2, 1))
        return v * cosf + sw * sinf

    cq = _dot(hb, win_ref[:, OFF_CQ:OFF_CKV])
    ckv = _dot(hb, win_ref[:, OFF_CKV:OFF_KR])
    kr = _dot(hb, win_ref[:, OFF_KR:OFF_HG])
    h4 = _dot(hb, win_ref[:, OFF_HG:OFF_GATE])
    cqn = _rms(cq, qn_ref[...]).astype(BF16)
    lat = _rms(ckv, kvn_ref[...])
    lat_ref[0] = lat.reshape(sb, tm, MLA_KV_RANK)
    latb = lat.astype(BF16)
    q = _dot(cqn, wuq_ref[...]) * qscale
    kn = _dot(latb, wuk_ref[...])
    if with_vt:
        vt = _dot_nt(wuvt_ref[...], latb).astype(BF16)
        ones = jnp.ones((VT_ROWS - MLA_V, rows), BF16)
        for h in range(MLA_HEADS):
            vt_ref[0, h * VT_ROWS:h * VT_ROWS + MLA_V, :] = vt[h * MLA_V:(h + 1) * MLA_V]
            vt_ref[0, h * VT_ROWS + MLA_V:(h + 1) * VT_ROWS, :] = ones
    graw = _dot(hb, win_ref[:, OFF_GATE:IN_COLS_R])

    kpe = rot(kr)
    kpe_ref[0] = kpe[:, :MLA_ROPE].reshape(sb, tm, MLA_ROPE)
    kpeb = kpe.astype(BF16)

    nope_w = MLA_HEADS * MLA_NOPE
    for g in range(MLA_HEADS // 2):
        r = rot(q[:, nope_w + g * LANES: nope_w + (g + 1) * LANES])
        for hh in range(2):
            h = 2 * g + hh
            rh = jnp.where(low_head if hh == 0 else jnp.logical_not(low_head), r, 0.0)
            qh = jnp.concatenate([q[:, h * MLA_NOPE:(h + 1) * MLA_NOPE], rh], axis=1).astype(BF16)
            kh = jnp.concatenate([kn[:, h * MLA_NOPE:(h + 1) * MLA_NOPE].astype(BF16), kpeb], axis=1)
            q_ref[:, h] = qh.reshape(sb, tm, QK_PAD)
            k_ref[:, h] = kh.reshape(sb, tm, QK_PAD)

    lg = lbl_ref[...]
    e = jnp.exp(lg - jnp.max(lg, axis=0, keepdims=True))
    sm = e / jnp.sum(e, axis=0, keepdims=True)
    lb = sm[0:1] * 0.0
    for i in range(1, layer + 1):
        lb = lb + sm[i:i + 1]
    hq, hf, hi, hg = (h4[:, k * HG_W:(k + 1) * HG_W] for k in range(4))
    h4 = jnp.concatenate([hq * _sigmoid(hq) * HG_DK ** -0.5, lb + (1.0 - lb) * _sigmoid(hf), hi, hg * _sigmoid(hg)],
                         axis=1)
    h4_ref[...] = h4.reshape(sb, tm, 4 * HG_W)
    gates = _sigmoid(graw)
    g_ref[...] = gates.astype(BF16).reshape(sb, tm, 2 * D_MODEL)


def _inproj(x, cosf, sinf, nmix, win, qn, wuq, kvn, wuk, wuvt, lbl, lat_buf, kpe_buf, layer, sb, tm, with_vt):
    B, L, _ = x.shape
    H = MLA_HEADS
    qscale = (1.0 / math.sqrt(MLA_QK)) * math.log2(math.e)
    full = _resident
    seq = lambda w: pl.BlockSpec((sb, tm, w), lambda b, t: (b, t, 0))
    stacked = lambda w: pl.BlockSpec((1, sb, tm, w), lambda b, t: (layer, b, t, 0))
    qk_shape = jax.ShapeDtypeStruct((B, H, L, QK_PAD), BF16)
    qk_spec = pl.BlockSpec((sb, H, tm, QK_PAD), lambda b, t: (b, 0, t, 0))
    out_shape = [qk_shape, qk_shape]
    out_specs = [qk_spec, qk_spec]
    if with_vt:
        assert sb == 1
        out_shape.append(jax.ShapeDtypeStruct((B, H * VT_ROWS, L), BF16))
        out_specs.append(pl.BlockSpec((1, H * VT_ROWS, tm), lambda b, t: (b, 0, t)))
    lat_idx = len(out_shape)
    out_shape += [jax.ShapeDtypeStruct(lat_buf.shape, F32), jax.ShapeDtypeStruct(kpe_buf.shape, F32),
                  jax.ShapeDtypeStruct((B, L, 4 * HG_W), F32), jax.ShapeDtypeStruct((B, L, 2 * D_MODEL), BF16)]
    out_specs += [stacked(MLA_KV_RANK), stacked(MLA_ROPE), seq(4 * HG_W), seq(2 * D_MODEL)]
    in_specs = [
        seq(D_MODEL),
        pl.BlockSpec((sb * tm, LANES), lambda b, t: (t, 0)),
        pl.BlockSpec((sb * tm, LANES), lambda b, t: (t, 0)),
        full(nmix.shape), full(win.shape), full(qn.shape), full(wuq.shape), full(kvn.shape),
        full(wuk.shape), full(wuvt.shape), full(lbl.shape),
        _aliased(), _aliased(),
    ]
    return pl.pallas_call(
        functools.partial(_inproj_kernel, qscale=qscale, layer=layer, sb=sb, tm=tm, with_vt=with_vt),
        out_shape=tuple(out_shape), grid=(B // sb, L // tm), in_specs=in_specs, out_specs=tuple(out_specs),
        input_output_aliases={11: lat_idx, 12: lat_idx + 1},
        compiler_params=_params("parallel", "arbitrary"), name="inproj",
    )(x, cosf, sinf, nmix, win, qn, wuq, kvn, wuk, wuvt, lbl, lat_buf, kpe_buf)


ATTN_TK = 256
ATTN_SHIFT_MAX = 64.0


def _attn_kernel(q_ref, k_ref, vt_ref, o_ref, acc_ref, *, seq, tk):
    nst = seq // tk
    kc = lax.broadcasted_iota(jnp.int32, (tk, tk), 0) // CHUNK
    qc = lax.broadcasted_iota(jnp.int32, (tk, tk), 1) // CHUNK
    visible = kc <= qc

    def finish(acc):
        return (acc[:MLA_V] / acc[MLA_V:MLA_V + 1]).T.astype(BF16)

    def scores(j):
        q0 = j * tk
        s = _dot_nt(k_ref[0, 0, q0:q0 + tk, :], q_ref[0, 0, q0:, :])
        sm = jnp.where(visible, s[:, :tk], -jnp.inf)
        return sm if s.shape[1] == tk else jnp.concatenate([sm, s[:, tk:]], axis=1)

    s = scores(0)
    m = jnp.max(s, axis=0, keepdims=True)
    acc_ref[...] = _dot(vt_ref[0, :, 0:tk], jnp.exp2(s - m).astype(BF16))
    excess = jnp.zeros((1, seq), F32)
    s_next = scores(1) if nst > 1 else None
    for j in range(1, nst):
        q0 = j * tk
        s = s_next
        if j + 1 < nst:
            s_next = scores(j + 1)
        mcur = m[:, q0:]
        p = jnp.exp2(s - mcur).astype(BF16)
        mx = jnp.max(s, axis=0, keepdims=True)
        m_new = jnp.maximum(mcur, mx)
        acc_ref[:, q0:] = jnp.exp2(mcur - m_new) * (acc_ref[:, q0:] + _dot(vt_ref[0, :, q0:q0 + tk], p))
        excess = jnp.concatenate([excess[:, :q0], jnp.maximum(excess[:, q0:], mx - mcur)], axis=1)
        m = jnp.concatenate([m[:, :q0], m_new], axis=1)
    for cb in range(nst):
        o_ref[0, cb * tk:(cb + 1) * tk, :] = finish(acc_ref[:, cb * tk:(cb + 1) * tk])

    @pl.when(jnp.max(excess) > ATTN_SHIFT_MAX)
    def _():
        def column_block(cb, carry):
            c0 = pl.multiple_of(cb * tk, tk)
            qb = q_ref[0, 0, pl.ds(c0, tk), :]
            qcol = (c0 + lax.broadcasted_iota(jnp.int32, (tk, tk), 1)) // CHUNK

            def key_block(j, mla):
                mo, acc = mla
                k0 = pl.multiple_of(j * tk, tk)
                sc = _dot_nt(k_ref[0, 0, pl.ds(k0, tk), :], qb)
                krow = (k0 + lax.broadcasted_iota(jnp.int32, (tk, tk), 0)) // CHUNK
                sc = jnp.where(krow <= qcol, sc, -jnp.inf)
                mn = jnp.maximum(mo, jnp.max(sc, axis=0, keepdims=True))
                pv = _dot(vt_ref[0, :, pl.ds(k0, tk)], jnp.exp2(sc - mn).astype(BF16))
                return mn, jnp.exp2(mo - mn) * acc + pv

            init = (jnp.full((1, tk), -jnp.inf, F32), jnp.zeros((VT_ROWS, tk), F32))
            _, acc = lax.fori_loop(0, cb + 1, key_block, init)
            o_ref[0, pl.ds(c0, tk), :] = finish(acc)
            return carry
        lax.fori_loop(0, nst, column_block, 0)


def _attn_prompt(q, k, vt, tk):
    B, H, L, _ = q.shape
    return pl.pallas_call(
        functools.partial(_attn_kernel, seq=L, tk=tk),
        out_shape=jax.ShapeDtypeStruct((B, L, H * MLA_V), BF16),
        grid=(B, H),
        in_specs=[
            pl.BlockSpec((1, 1, L, QK_PAD), lambda b, h: (b, h, 0, 0)),
            pl.BlockSpec((1, 1, L, QK_PAD), lambda b, h: (b, h, 0, 0)),
            pl.BlockSpec((1, VT_ROWS, L), lambda b, h: (b, h, 0)),
        ],
        out_specs=pl.BlockSpec((1, L, MLA_V), lambda b, h: (b, 0, h)),
        scratch_shapes=[pltpu.VMEM((VT_ROWS, L), F32)],
        compiler_params=_params("parallel", "parallel"), name="attn_prompt",
    )(q, k, vt)


def _attn_sample_kernel(q_ref, kn_ref, latn_ref, latp_ref, kpep_ref, wuk_ref, wuv_ref, o_ref):
    latp = latp_ref[0].astype(BF16)
    latn = latn_ref[0, 0].astype(BF16)
    knp = _dot(latp, wuk_ref[...]).astype(BF16)
    vp = _dot(latp, wuv_ref[...]).astype(BF16)
    vn = _dot(latn, wuv_ref[...]).astype(BF16)
    kp = kpep_ref[0].astype(BF16)
    kpd = jnp.concatenate([kp, kp], axis=1)
    for h in range(MLA_HEADS):
        sl = slice(h * MLA_NOPE, (h + 1) * MLA_NOPE)
        qh = q_ref[0, h]
        khp = jnp.concatenate([knp[:, sl], kpd], axis=1)
        s1 = _dot_nt(qh, khp)
        s2 = _dot_nt(qh, kn_ref[0, h])
        m = jnp.maximum(jnp.max(s1, axis=1, keepdims=True), jnp.max(s2, axis=1, keepdims=True))
        p1 = jnp.exp2(s1 - m)
        p2 = jnp.exp2(s2 - m)
        l = jnp.sum(p1, axis=1, keepdims=True) + jnp.sum(p2, axis=1, keepdims=True)
        o = _dot(p1.astype(BF16), vp[:, sl]) + _dot(p2.astype(BF16), vn[:, sl])
        o_ref[0, :, sl] = (o / l).astype(BF16)


def _attn_sample(q, kn, lat_buf, layer, latp, kpep, wuk, wuv):
    B, H, Ls, _ = q.shape
    P = latp.shape[1]
    full = _resident
    return pl.pallas_call(
        _attn_sample_kernel,
        out_shape=jax.ShapeDtypeStruct((B, Ls, H * MLA_V), BF16),
        grid=(B,),
        in_specs=[
            pl.BlockSpec((1, H, Ls, QK_PAD), lambda b: (b, 0, 0, 0)),
            pl.BlockSpec((1, H, Ls, QK_PAD), lambda b: (b, 0, 0, 0)),
            pl.BlockSpec((1, 1, Ls, MLA_KV_RANK), lambda b: (layer, b, 0, 0)),
            pl.BlockSpec((1, P, MLA_KV_RANK), lambda b: (b, 0, 0)),
            pl.BlockSpec((1, P, MLA_ROPE), lambda b: (b, 0, 0)),
            full(wuk.shape), full(wuv.shape),
        ],
        out_specs=pl.BlockSpec((1, Ls, H * MLA_V), lambda b: (b, 0, 0)),
        compiler_params=_params("parallel"), name="attn_sample",
    )(q, kn, lat_buf, latp, kpep, wuk, wuv)


DIAG_BLOCK = SUBLANES
HGRN_UNROLL = 8


def _hgrn_kernel(h4_ref, gn_ref, s0_ref, sbuf_in, o_ref, sout_ref, st_ref, pb_ref, ob_ref, lvl_ref,
                 *, c, nchunk, unroll):
    del sbuf_in
    t = pl.program_id(1)

    @pl.when(t == 0)
    def _():
        for h in range(HG_HEADS):
            st_ref[h] = s0_ref[0, h].T
        rt4 = lax.broadcasted_iota(jnp.int32, (HG_HEADS * c, HG_HEADS * c), 0)
        ct4 = lax.broadcasted_iota(jnp.int32, (HG_HEADS * c, HG_HEADS * c), 1)
        lvl = jnp.zeros((HG_HEADS * c, HG_HEADS * c), jnp.int32)
        n, level = c // 2, 0
        while n >= DIAG_BLOCK:
            level += 1
            hit = ((rt4 // (2 * n)) == (ct4 // (2 * n))) & ((rt4 % (2 * n)) >= n) & ((ct4 % (2 * n)) < n)
            lvl = jnp.where(hit, level, lvl)
            n //= 2
        lvl_ref[...] = lvl

    gn = gn_ref[...]

    rt = lax.broadcasted_iota(jnp.int32, (c, c), 0)
    ct = lax.broadcasted_iota(jnp.int32, (c, c), 1)
    tril = jnp.where(ct <= rt, 1.0, 0.0).astype(BF16)
    tril3 = jnp.concatenate([tril, tril, tril], axis=1)
    nblk = c // DIAG_BLOCK

    def chunk(ci, u):
        r0 = pl.multiple_of(ci * c, c)

        def seg(k):
            return h4_ref[0, pl.ds(r0, c), pl.ds(k * HG_W, HG_W)]

        def stack(w):
            return jnp.concatenate([w[:, h * HG_DK:(h + 1) * HG_DK] for h in range(HG_HEADS)], axis=0)

        qq_w, f_w, hi_w, gs_w = seg(0), seg(1), seg(2), seg(3)
        logf = jnp.log(f_w)
        kk_w = 1.0 - f_w
        vb_w = hi_w.astype(BF16)

        p0 = logf.astype(BF16)
        r1 = logf - p0.astype(F32)
        p1 = r1.astype(BF16)
        p2 = (r1 - p1.astype(F32)).astype(BF16)
        b_w = _dot(tril3, jnp.concatenate([p0, p1, p2], axis=0))
        bend_w = b_w[c - 1:c, :]
        ebend_w = jnp.exp(bend_w)

        inter = _dot_nt(stack((qq_w * jnp.exp(b_w)).astype(BF16)),
                        st_ref[...].reshape(HG_HEADS * HG_DV, HG_DK).astype(BF16))
        kd_w = (kk_w * jnp.exp(bend_w - b_w)).astype(BF16)
        upd = lax.dot_general(vb_w, kd_w, TN_DIMS, preferred_element_type=F32)
        for h in range(HG_HEADS):
            hs = slice(h * HG_DK, (h + 1) * HG_DK)
            st_ref[h] = st_ref[h] * ebend_w[:, hs] + upd[hs, hs]

        a_all = jnp.zeros((HG_HEADS * c, HG_HEADS * c), F32)
        n = c // 2
        level = 0
        while n >= DIAG_BLOCK:
            qparts, kparts = [], []
            for j in range(c // (2 * n)):
                lo, mid, hi_ = 2 * n * j, 2 * n * j + n, 2 * n * (j + 1)
                ref = b_w[mid - 1:mid, :]
                qparts += [jnp.zeros((n, HG_W), F32), qq_w[mid:hi_] * jnp.exp(b_w[mid:hi_] - ref)]
                kparts += [kk_w[lo:mid] * jnp.exp(ref - b_w[lo:mid]), jnp.zeros((n, HG_W), F32)]
            an = _dot_nt(stack(jnp.concatenate(qparts, axis=0)).astype(BF16),
                         stack(jnp.concatenate(kparts, axis=0)).astype(BF16))
            level += 1
            a_all = jnp.where(lvl_ref[...] == level, an, a_all)
            n //= 2
        intra = _dot(a_all.astype(BF16), stack(vb_w))

        for h in range(HG_HEADS):
            sl = u * HG_HEADS + h
            hs = slice(h * HG_DK, (h + 1) * HG_DK)
            qq, kk, b, hi = qq_w[:, hs], kk_w[:, hs], b_w[:, hs], hi_w[:, hs]
            o = inter[h * c:(h + 1) * c, hs] + intra[h * c:(h + 1) * c]

            pb_ref[sl, 0] = qq
            pb_ref[sl, 1] = kk
            pb_ref[sl, 2] = b
            pb_ref[sl, 3] = hi
            rows = [[pb_ref[sl, k, pl.ds(a, nblk, stride=DIAG_BLOCK), :] for a in range(DIAG_BLOCK)] for k in range(4)]
            for a in range(DIAG_BLOCK):
                oa = jnp.zeros((nblk, HG_DV), F32)
                for e_ in range(a + 1):
                    w = rows[0][a] * rows[1][e_]
                    if e_ < a:
                        w = w * jnp.exp(rows[2][a] - rows[2][e_])
                    oa = oa + jnp.sum(w, axis=1, keepdims=True) * rows[3][e_]
                ob_ref[sl, pl.ds(a, nblk, stride=DIAG_BLOCK), :] = oa
            o = o + ob_ref[sl]

            ob = _rms(o, gn) * gs_w[:, hs]
            o_ref[0, pl.ds(r0, c), pl.ds(h * HG_DV, HG_DV)] = ob.astype(BF16)

    def unrolled(cu, carry):
        for u in range(unroll):
            chunk(cu * unroll + u, u)
        return carry

    lax.fori_loop(0, nchunk // unroll, unrolled, 0)

    @pl.when(t == pl.num_programs(1) - 1)
    def _():
        for h in range(HG_HEADS):
            sout_ref[0, 0, h] = st_ref[h].T


def _hgrn(h4, gnorm, s0, s_buf, layer, tt):
    B, L, _ = h4.shape
    c = min(CHUNK, L)
    tt = min(tt, L)
    nchunk = tt // c
    unroll = HGRN_UNROLL if nchunk % HGRN_UNROLL == 0 else 1
    full = _resident
    return pl.pallas_call(
        functools.partial(_hgrn_kernel, c=c, nchunk=nchunk, unroll=unroll),
        out_shape=(jax.ShapeDtypeStruct((B, L, HG_HEADS * HG_DV), BF16),
                   jax.ShapeDtypeStruct(s_buf.shape, F32)),
        grid=(B, L // tt),
        in_specs=[
            pl.BlockSpec((1, tt, 4 * HG_W), lambda b, t: (b, t, 0)),
            full(gnorm.shape),
            pl.BlockSpec((1, HG_HEADS, HG_DK, HG_DV), lambda b, t: (b, 0, 0, 0)),
            _aliased(),
        ],
        out_specs=(pl.BlockSpec((1, tt, HG_HEADS * HG_DV), lambda b, t: (b, t, 0)),
                   pl.BlockSpec((1, 1, HG_HEADS, HG_DK, HG_DV), lambda b, t: (layer, b, 0, 0, 0))),
        scratch_shapes=[pltpu.VMEM((HG_HEADS, HG_DV, HG_DK), F32),
                        pltpu.VMEM((unroll * HG_HEADS, 4, c, HG_DK), F32),
                        pltpu.VMEM((unroll * HG_HEADS, c, HG_DV), F32),
                        pltpu.VMEM((HG_HEADS * c, HG_HEADS * c), jnp.int32)],
        input_output_aliases={3: 1},
        compiler_params=_params("parallel", "arbitrary"), name="hgrn",
    )(h4, gnorm, s0, s_buf)


def _merge_kernel(x_ref, oa_ref, ob_ref, g_ref, wpa_ref, wpb_ref, wout_ref, y_ref, *, sb, tm):
    rows = sb * tm
    g = g_ref[...].reshape(rows, 2 * D_MODEL).astype(F32)
    oa = oa_ref[...].reshape(rows, MLA_HEADS * MLA_V)
    ob = ob_ref[...].reshape(rows, HG_HEADS * HG_DV)
    mix = g[:, :D_MODEL] * _dot(oa, wpa_ref[...]) + g[:, D_MODEL:] * _dot(ob, wpb_ref[...])
    y = x_ref[...].reshape(rows, D_MODEL) + _dot(mix.astype(BF16), wout_ref[...])
    y_ref[...] = y.reshape(sb, tm, D_MODEL)


def _merge(x, oa, ob, g, wpa, wpb, wout, sb, tm):
    B, L, _ = x.shape
    full = _resident
    row = lambda w: pl.BlockSpec((sb, tm, w), lambda b, t: (b, t, 0))
    return pl.pallas_call(
        functools.partial(_merge_kernel, sb=sb, tm=tm),
        out_shape=jax.ShapeDtypeStruct(x.shape, F32),
        grid=(B // sb, L // tm),
        in_specs=[row(D_MODEL), row(oa.shape[2]), row(ob.shape[2]), row(2 * D_MODEL),
                  full(wpa.shape), full(wpb.shape), full(wout.shape)],
        out_specs=row(D_MODEL),
        compiler_params=_params("parallel", "parallel"), name="merge",
    )(x, oa, ob, g, wpa, wpb, wout)


FFN_SPLIT = 2
FFN_COLS = FFN_DIM // FFN_SPLIT


def _gelu_tanh(x):
    return 0.5 * x * (1.0 + jnp.tanh(math.sqrt(2.0 / math.pi) * (x + 0.044715 * (x * x * x))))


def _ffn_kernel(x_ref, cs_ref, nf_ref, wup_ref, cw_ref, cb_ref, wdn_ref, nfin_ref, cbuf_in, y_ref, cnew_ref,
                carry_ref, *, final, sb, tm):
    del cbuf_in
    t = pl.program_id(1)
    rows = sb * tm

    @pl.when(t == 0)
    def _():
        carry_ref[:, 0:CONV_W - 1, :] = cs_ref[...]

    x = x_ref[...].reshape(rows, D_MODEL)
    hb = _rms(x, nf_ref[...]).astype(BF16)
    pos = lax.broadcasted_iota(jnp.int32, (sb, tm, FFN_COLS), 1).reshape(rows, FFN_COLS)

    def per_row(v):
        return jnp.broadcast_to(v, (sb, tm, FFN_COLS)).reshape(rows, FFN_COLS)

    acc = x
    for c in range(FFN_SPLIT):
        cs = slice(c * FFN_COLS, (c + 1) * FFN_COLS)
        a = _dot(hb, wup_ref[:, c * FFN_COLS:(c + 1) * FFN_COLS])
        val = _dot(hb, wup_ref[:, FFN_DIM + c * FFN_COLS:FFN_DIM + (c + 1) * FFN_COLS])
        prev2 = per_row(carry_ref[:, 0:1, cs])
        prev1 = per_row(carry_ref[:, 1:2, cs])
        a1 = jnp.where(pos == 0, prev1, pltpu.roll(a, 1, 0))
        a2 = jnp.where(pos == 0, prev2, jnp.where(pos == 1, prev1, pltpu.roll(a, 2, 0)))
        conv = cb_ref[:, cs] + cw_ref[0:1, cs] * a2 + cw_ref[1:2, cs] * a1 + cw_ref[2:3, cs] * a
        gated = (_gelu_tanh(conv) * val).astype(BF16)
        acc = acc + _dot(gated, wdn_ref[c * FFN_COLS:(c + 1) * FFN_COLS, :])
        carry_ref[:, 0:CONV_W - 1, cs] = a.reshape(sb, tm, FFN_COLS)[:, tm - (CONV_W - 1):tm, :]
    if final:
        acc = _rms(acc, nfin_ref[...])
    y_ref[...] = acc.reshape(sb, tm, D_MODEL)

    @pl.when(t == pl.num_programs(1) - 1)
    def _():
        cnew_ref[0] = carry_ref[:, 0:CONV_W - 1, :]


def _ffn(x, conv_state, nf, wup, cw, cb, wdn, nfin, c_buf, layer, final, sb, tm):
    B, L, _ = x.shape
    full = _resident
    return pl.pallas_call(
        functools.partial(_ffn_kernel, final=final, sb=sb, tm=tm),
        out_shape=(jax.ShapeDtypeStruct(x.shape, F32), jax.ShapeDtypeStruct(c_buf.shape, F32)),
        grid=(B // sb, L // tm),
        in_specs=[
            pl.BlockSpec((sb, tm, D_MODEL), lambda b, t: (b, t, 0)),
            pl.BlockSpec((sb, CONV_W - 1, FFN_DIM), lambda b, t: (b, 0, 0)),
            full(nf.shape), full(wup.shape), full(cw.shape), full(cb.shape), full(wdn.shape), full(nfin.shape),
            _aliased(),
        ],
        out_specs=(pl.BlockSpec((sb, tm, D_MODEL), lambda b, t: (b, t, 0)),
                   pl.BlockSpec((1, sb, CONV_W - 1, FFN_DIM), lambda b, t: (layer, b, 0, 0))),
        scratch_shapes=[pltpu.VMEM((sb, SUBLANES, FFN_DIM), F32)],
        input_output_aliases={8: 1},
        compiler_params=_params("parallel", "arbitrary"), name="ffn",
    )(x, conv_state, nf, wup, cw, cb, wdn, nfin, c_buf)


def _rope_tables(pos):
    half = MLA_ROPE // 2
    inv = ROPE_BASE ** (-jnp.arange(half, dtype=F32) / half)
    ang = pos.astype(F32)[:, None] * inv[None, :]
    cos, sin = jnp.cos(ang), jnp.sin(ang)
    return jnp.tile(cos, (1, 4)), jnp.concatenate([-sin, sin, -sin, sin], axis=1)


def _prep_layer(w_in_l, w_uq_l, w_ukv_l):
    kr = w_in_l[:, OFF_KR:OFF_KR + MLA_ROPE]
    win = jnp.concatenate([w_in_l[:, :OFF_KR], kr, kr, w_in_l[:, OFF_KR + MLA_ROPE:]], axis=1).astype(BF16)
    uq = w_uq_l.reshape(MLA_Q_RANK, MLA_HEADS, MLA_QK)
    wuq = jnp.concatenate([uq[:, :, :MLA_NOPE].reshape(MLA_Q_RANK, -1),
                           uq[:, :, MLA_NOPE:].reshape(MLA_Q_RANK, -1)], axis=1).astype(BF16)
    ukv = w_ukv_l.reshape(MLA_KV_RANK, MLA_HEADS, MLA_NOPE + MLA_V)
    wuk = ukv[:, :, :MLA_NOPE].reshape(MLA_KV_RANK, -1).astype(BF16)
    wuv = ukv[:, :, MLA_NOPE:].reshape(MLA_KV_RANK, -1).astype(BF16)
    return win, wuq, wuk, wuv


def _tiling(B, L):
    if L % ROW_TILE == 0:
        return 1, ROW_TILE
    sb = 1
    if L % SUBLANES == 0:
        for cand in range(1, B + 1):
            if B % cand == 0 and cand * L <= ROW_TILE:
                sb = cand
    return sb, L


def _state_buffers(depth, B, L):
    return (jnp.zeros((depth, B, L, MLA_KV_RANK), F32), jnp.zeros((depth, B, L, MLA_ROPE), F32),
            jnp.zeros((depth, B, HG_HEADS, HG_DK, HG_DV), F32), jnp.zeros((depth, B, CONV_W - 1, FFN_DIM), F32))


def kernel(x_prompt, x_sample, cache_mla_latent, cache_mla_krope, state_hgrn, state_ffn_conv, norm_mix, w_in, q_norm, w_uq, kv_norm, w_ukv, lb_logits, hgrn_norm, w_proj_a, w_proj_b, w_out, norm_ffn, w_up, conv_w, conv_b, w_down, norm_final):
    depth = w_in.shape[0]
    B, Lp, _ = x_prompt.shape
    Bs, Ls, _ = x_sample.shape
    past = cache_mla_latent.shape[2]

    sb_p, tm_p = _tiling(B, Lp)
    sb_s, tm_s = _tiling(Bs, Ls)
    cos_p, sin_p = _rope_tables(jnp.arange(Lp))
    cos_s, sin_s = _rope_tables(past + jnp.arange(Ls))
    cos_s, sin_s = jnp.tile(cos_s, (sb_s, 1)), jnp.tile(sin_s, (sb_s, 1))
    s0_p = jnp.zeros((B, HG_HEADS, HG_DK, HG_DV), F32)
    conv0_p = jnp.zeros((B, CONV_W - 1, FFN_DIM), F32)
    nfin = norm_final.reshape(1, D_MODEL)
    lbl = lb_logits.astype(F32)

    tk = ATTN_TK if Lp % ATTN_TK == 0 else Lp

    lat_p, kpe_p, hs_p, cv_p = _state_buffers(depth, B, Lp)
    lat_s, kpe_s, hs_s, cv_s = _state_buffers(depth, Bs, Ls)

    xp, xs = x_prompt, x_sample
    for l in range(depth):
        win, wuq, wuk, wuv = _prep_layer(w_in[l], w_uq[l], w_ukv[l])
        wuvt = wuv.T
        nmix = norm_mix[l].reshape(1, -1)
        qn = q_norm[l].reshape(1, -1)
        kvn = kv_norm[l].reshape(1, -1)
        gn = hgrn_norm[l].reshape(1, -1)
        nf = norm_ffn[l].reshape(1, -1)
        wpa, wpb, wo = w_proj_a[l].astype(BF16), w_proj_b[l].astype(BF16), w_out[l].astype(BF16)
        wup, wdn = w_up[l].astype(BF16), w_down[l].astype(BF16)
        cw, cb = conv_w[l], conv_b[l].reshape(1, -1)
        final = l == depth - 1

        q, k, vt, lat_p, kpe_p, h4, g = _inproj(xp, cos_p, sin_p, nmix, win, qn, wuq, kvn, wuk, wuvt, lbl,
                                                lat_p, kpe_p, l, sb_p, tm_p, True)
        oa = _attn_prompt(q, k, vt, tk)
        ob, hs_p = _hgrn(h4, gn, s0_p, hs_p, l, 512)
        x1 = _merge(xp, oa, ob, g, wpa, wpb, wo, sb_p, tm_p)
        xp, cv_p = _ffn(x1, conv0_p, nf, wup, cw, cb, wdn, nfin, cv_p, l, final, sb_p, tm_p)

        q, k, lat_s, kpe_s, h4, g = _inproj(xs, cos_s, sin_s, nmix, win, qn, wuq, kvn, wuk, wuvt, lbl,
                                            lat_s, kpe_s, l, sb_s, tm_s, False)
        oa = _attn_sample(q, k, lat_s, l, cache_mla_latent[l], cache_mla_krope[l], wuk, wuv)
        ob, hs_s = _hgrn(h4, gn, state_hgrn[l], hs_s, l, 512)
        x1 = _merge(xs, oa, ob, g, wpa, wpb, wo, sb_s, tm_s)
        xs, cv_s = _ffn(x1, state_ffn_conv[l], nf, wup, cw, cb, wdn, nfin, cv_s, l, final, sb_s, tm_s)

    return (xp, xs, lat_p, kpe_p, hs_p, cv_p, lat_s, kpe_s, hs_s, cv_s)
```

```python
import functools
import math

import numpy as np
import jax
import jax.numpy as jnp
from jax import lax
from jax.experimental import pallas as pl
from jax.experimental.pallas import tpu as pltpu

D_MODEL = 1024
CHUNK = 64
EPS = 1e-6
MLA_HEADS = 8
MLA_Q_RANK = 384
MLA_KV_RANK = 256
MLA_NOPE = 128
MLA_ROPE = 64
MLA_V = 128
ROPE_BASE = 10000.0
HG_HEADS = 4
HG_DK = 128
HG_DV = 128
FFN_DIM = 2816
CONV_W = 3

MLA_QK = MLA_NOPE + MLA_ROPE
HG_W = HG_HEADS * HG_DK
QK_PAD = 2 * MLA_NOPE
VT_ROWS = MLA_V + 16
LANES = 128
SUBLANES = 8
VMEM_LIMIT = 56 * 1024 * 1024
ROW_TILE = 512

OFF_CQ = 0
OFF_CKV = OFF_CQ + MLA_Q_RANK
OFF_KR = OFF_CKV + MLA_KV_RANK
OFF_HG = OFF_KR + 2 * MLA_ROPE
OFF_GATE = OFF_HG + 4 * HG_W
IN_COLS_R = OFF_GATE + 2 * D_MODEL

BF16 = jnp.bfloat16
F32 = jnp.float32
NT_DIMS = (((1,), (1,)), ((), ()))
TN_DIMS = (((0,), (0,)), ((), ()))


def _dot(a, b):
    return jnp.dot(a, b, preferred_element_type=F32)


def _dot_nt(a, b):
    return lax.dot_general(a, b, NT_DIMS, preferred_element_type=F32)


def _rms(x, g):
    return x * lax.rsqrt(jnp.mean(x * x, axis=-1, keepdims=True) + EPS) * g


def _sigmoid(x):
    return 0.5 * (jnp.tanh(0.5 * x) + 1.0)


def _resident(shape):
    return pl.BlockSpec(shape, lambda *_: (0,) * len(shape), pipeline_mode=pl.Buffered(1))


def _aliased():
    return pl.BlockSpec(memory_space=pl.ANY)


def _params(*sem):
    return pltpu.CompilerParams(dimension_semantics=sem, vmem_limit_bytes=VMEM_LIMIT)


def _inproj_kernel(x_ref, cos_ref, sin_ref, nmix_ref, win_ref, qn_ref, wuq_ref, kvn_ref, wuk_ref, wuvt_ref, lbl_ref,
                   lat_in, kpe_in, *out_refs, qscale, layer, sb, tm, with_vt):
    del lat_in, kpe_in
    if with_vt:
        q_ref, k_ref, vt_ref, lat_ref, kpe_ref, h4_ref, g_ref = out_refs
    else:
        q_ref, k_ref, lat_ref, kpe_ref, h4_ref, g_ref = out_refs
    rows = sb * tm
    x = x_ref[...].reshape(rows, D_MODEL)
    hb = _rms(x, nmix_ref[...]).astype(BF16)
    cosf = cos_ref[...]
    sinf = sin_ref[...]
    lane = lax.broadcasted_iota(jnp.int32, cosf.shape, 1)
    first_half = (lane % MLA_ROPE) < (MLA_ROPE // 2)
    low_head = lane < MLA_ROPE

    def rot(v):
        sw = jnp.where(first_half, pltpu.roll(v, LANES - MLA_ROPE // 2, 1), pltpu.roll(v, MLA_ROPE // 2, 1))
        return v * cosf + sw * sinf

    cq = _dot(hb, win_ref[:, OFF_CQ:OFF_CKV])
    ckv = _dot(hb, win_ref[:, OFF_CKV:OFF_KR])
    kr = _dot(hb, win_ref[:, OFF_KR:OFF_HG])
    h4 = _dot(hb, win_ref[:, OFF_HG:OFF_GATE])
    cqn = _rms(cq, qn_ref[...]).astype(BF16)
    lat = _rms(ckv, kvn_ref[...])
    lat_ref[0] = lat.reshape(sb, tm, MLA_KV_RANK)
    latb = lat.astype(BF16)
    q = _dot(cqn, wuq_ref[...]) * qscale
    kn = _dot(latb, wuk_ref[...])
    if with_vt:
        vt = _dot_nt(wuvt_ref[...], latb).astype(BF16)
        ones = jnp.ones((VT_ROWS - MLA_V, rows), BF16)
        for h in range(MLA_HEADS):
            vt_ref[0, h * VT_ROWS:h * VT_ROWS + MLA_V, :] = vt[h * MLA_V:(h + 1) * MLA_V]
            vt_ref[0, h * VT_ROWS + MLA_V:(h + 1) * VT_ROWS, :] = ones
    graw = _dot(hb, win_ref[:, OFF_GATE:IN_COLS_R])

    kpe = rot(kr)
    kpe_ref[0] = kpe[:, :MLA_ROPE].reshape(sb, tm, MLA_ROPE)
    kpeb = kpe.astype(BF16)

    nope_w = MLA_HEADS * MLA_NOPE
    for g in range(MLA_HEADS // 2):
        r = rot(q[:, nope_w + g * LANES: nope_w + (g + 1) * LANES])
        for hh in range(2):
            h = 2 * g + hh
            rh = jnp.where(low_head if hh == 0 else jnp.logical_not(low_head), r, 0.0)
            qh = jnp.concatenate([q[:, h * MLA_NOPE:(h + 1) * MLA_NOPE], rh], axis=1).astype(BF16)
            kh = jnp.concatenate([kn[:, h * MLA_NOPE:(h + 1) * MLA_NOPE].astype(BF16), kpeb], axis=1)
            q_ref[:, h] = qh.reshape(sb, tm, QK_PAD)
            k_ref[:, h] = kh.reshape(sb, tm, QK_PAD)

    lg = lbl_ref[...]
    e = jnp.exp(lg - jnp.max(lg, axis=0, keepdims=True))
    sm = e / jnp.sum(e, axis=0, keepdims=True)
    lb = sm[0:1] * 0.0
    for i in range(1, layer + 1):
        lb = lb + sm[i:i + 1]
    hq, hf, hi, hg = (h4[:, k * HG_W:(k + 1) * HG_W] for k in range(4))
    h4 = jnp.concatenate([hq * _sigmoid(hq) * HG_DK ** -0.5, lb + (1.0 - lb) * _sigmoid(hf), hi, hg * _sigmoid(hg)],
                         axis=1)
    h4_ref[...] = h4.reshape(sb, tm, 4 * HG_W)
    gates = _sigmoid(graw)
    g_ref[...] = gates.astype(BF16).reshape(sb, tm, 2 * D_MODEL)


def _inproj(x, cosf, sinf, nmix, win, qn, wuq, kvn, wuk, wuvt, lbl, lat_buf, kpe_buf, layer, sb, tm, with_vt):
    B, L, _ = x.shape
    H = MLA_HEADS
    qscale = (1.0 / math.sqrt(MLA_QK)) * math.log2(math.e)
    full = _resident
    seq = lambda w: pl.BlockSpec((sb, tm, w), lambda b, t: (b, t, 0))
    stacked = lambda w: pl.BlockSpec((1, sb, tm, w), lambda b, t: (layer, b, t, 0))
    qk_shape = jax.ShapeDtypeStruct((B, H, L, QK_PAD), BF16)
    qk_spec = pl.BlockSpec((sb, H, tm, QK_PAD), lambda b, t: (b, 0, t, 0))
    out_shape = [qk_shape, qk_shape]
    out_specs = [qk_spec, qk_spec]
    if with_vt:
        assert sb == 1
        out_shape.append(jax.ShapeDtypeStruct((B, H * VT_ROWS, L), BF16))
        out_specs.append(pl.BlockSpec((1, H * VT_ROWS, tm), lambda b, t: (b, 0, t)))
    lat_idx = len(out_shape)
    out_shape += [jax.ShapeDtypeStruct(lat_buf.shape, F32), jax.ShapeDtypeStruct(kpe_buf.shape, F32),
                  jax.ShapeDtypeStruct((B, L, 4 * HG_W), F32), jax.ShapeDtypeStruct((B, L, 2 * D_MODEL), BF16)]
    out_specs += [stacked(MLA_KV_RANK), stacked(MLA_ROPE), seq(4 * HG_W), seq(2 * D_MODEL)]
    in_specs = [
        seq(D_MODEL),
        pl.BlockSpec((sb * tm, LANES), lambda b, t: (t, 0)),
        pl.BlockSpec((sb * tm, LANES), lambda b, t: (t, 0)),
        full(nmix.shape), full(win.shape), full(qn.shape), full(wuq.shape), full(kvn.shape),
        full(wuk.shape), full(wuvt.shape), full(lbl.shape),
        _aliased(), _aliased(),
    ]
    return pl.pallas_call(
        functools.partial(_inproj_kernel, qscale=qscale, layer=layer, sb=sb, tm=tm, with_vt=with_vt),
        out_shape=tuple(out_shape), grid=(B // sb, L // tm), in_specs=in_specs, out_specs=tuple(out_specs),
        input_output_aliases={11: lat_idx, 12: lat_idx + 1},
        compiler_params=_params("parallel", "arbitrary"), name="inproj",
    )(x, cosf, sinf, nmix, win, qn, wuq, kvn, wuk, wuvt, lbl, lat_buf, kpe_buf)


ATTN_TK = 256
ATTN_SHIFT_MAX = 64.0


def _attn_kernel(q_ref, k_ref, vt_ref, o_ref, acc_ref, *, seq, tk):
    nst = seq // tk
    kc = lax.broadcasted_iota(jnp.int32, (tk, tk), 0) // CHUNK
    qc = lax.broadcasted_iota(jnp.int32, (tk, tk), 1) // CHUNK
    visible = kc <= qc

    def finish(acc):
        return (acc[:MLA_V] / acc[MLA_V:MLA_V + 1]).T.astype(BF16)

    def scores(j):
        q0 = j * tk
        s = _dot_nt(k_ref[0, 0, q0:q0 + tk, :], q_ref[0, 0, q0:, :])
        sm = jnp.where(visible, s[:, :tk], -jnp.inf)
        return sm if s.shape[1] == tk else jnp.concatenate([sm, s[:, tk:]], axis=1)

    s = scores(0)
    m = jnp.max(s, axis=0, keepdims=True)
    acc_ref[...] = _dot(vt_ref[0, :, 0:tk], jnp.exp2(s - m).astype(BF16))
    excess = jnp.zeros((1, seq), F32)
    s_next = scores(1) if nst > 1 else None
    for j in range(1, nst):
        q0 = j * tk
        s = s_next
        if j + 1 < nst:
            s_next = scores(j + 1)
        mcur = m[:, q0:]
        p = jnp.exp2(s - mcur).astype(BF16)
        mx = jnp.max(s, axis=0, keepdims=True)
        m_new = jnp.maximum(mcur, mx)
        acc_ref[:, q0:] = jnp.exp2(mcur - m_new) * (acc_ref[:, q0:] + _dot(vt_ref[0, :, q0:q0 + tk], p))
        excess = jnp.concatenate([excess[:, :q0], jnp.maximum(excess[:, q0:], mx - mcur)], axis=1)
        m = jnp.concatenate([m[:, :q0], m_new], axis=1)
    for cb in range(nst):
        o_ref[0, cb * tk:(cb + 1) * tk, :] = finish(acc_ref[:, cb * tk:(cb + 1) * tk])

    @pl.when(jnp.max(excess) > ATTN_SHIFT_MAX)
    def _():
        def column_block(cb, carry):
            c0 = pl.multiple_of(cb * tk, tk)
            qb = q_ref[0, 0, pl.ds(c0, tk), :]
            qcol = (c0 + lax.broadcasted_iota(jnp.int32, (tk, tk), 1)) // CHUNK

            def key_block(j, mla):
                mo, acc = mla
                k0 = pl.multiple_of(j * tk, tk)
                sc = _dot_nt(k_ref[0, 0, pl.ds(k0, tk), :], qb)
                krow = (k0 + lax.broadcasted_iota(jnp.int32, (tk, tk), 0)) // CHUNK
                sc = jnp.where(krow <= qcol, sc, -jnp.inf)
                mn = jnp.maximum(mo, jnp.max(sc, axis=0, keepdims=True))
                pv = _dot(vt_ref[0, :, pl.ds(k0, tk)], jnp.exp2(sc - mn).astype(BF16))
                return mn, jnp.exp2(mo - mn) * acc + pv

            init = (jnp.full((1, tk), -jnp.inf, F32), jnp.zeros((VT_ROWS, tk), F32))
            _, acc = lax.fori_loop(0, cb + 1, key_block, init)
            o_ref[0, pl.ds(c0, tk), :] = finish(acc)
            return carry
        lax.fori_loop(0, nst, column_block, 0)


def _attn_prompt(q, k, vt, tk):
    B, H, L, _ = q.shape
    return pl.pallas_call(
        functools.partial(_attn_kernel, seq=L, tk=tk),
        out_shape=jax.ShapeDtypeStruct((B, L, H * MLA_V), BF16),
        grid=(B, H),
        in_specs=[
            pl.BlockSpec((1, 1, L, QK_PAD), lambda b, h: (b, h, 0, 0)),
            pl.BlockSpec((1, 1, L, QK_PAD), lambda b, h: (b, h, 0, 0)),
            pl.BlockSpec((1, VT_ROWS, L), lambda b, h: (b, h, 0)),
        ],
        out_specs=pl.BlockSpec((1, L, MLA_V), lambda b, h: (b, 0, h)),
        scratch_shapes=[pltpu.VMEM((VT_ROWS, L), F32)],
        compiler_params=_params("parallel", "parallel"), name="attn_prompt",
    )(q, k, vt)


def _attn_sample_kernel(q_ref, kn_ref, latn_ref, latp_ref, kpep_ref, wuk_ref, wuv_ref, o_ref):
    latp = latp_ref[0].astype(BF16)
    latn = latn_ref[0, 0].astype(BF16)
    knp = _dot(latp, wuk_ref[...]).astype(BF16)
    vp = _dot(latp, wuv_ref[...]).astype(BF16)
    vn = _dot(latn, wuv_ref[...]).astype(BF16)
    kp = kpep_ref[0].astype(BF16)
    kpd = jnp.concatenate([kp, kp], axis=1)
    for h in range(MLA_HEADS):
        sl = slice(h * MLA_NOPE, (h + 1) * MLA_NOPE)
        qh = q_ref[0, h]
        khp = jnp.concatenate([knp[:, sl], kpd], axis=1)
        s1 = _dot_nt(qh, khp)
        s2 = _dot_nt(qh, kn_ref[0, h])
        m = jnp.maximum(jnp.max(s1, axis=1, keepdims=True), jnp.max(s2, axis=1, keepdims=True))
        p1 = jnp.exp2(s1 - m)
        p2 = jnp.exp2(s2 - m)
        l = jnp.sum(p1, axis=1, keepdims=True) + jnp.sum(p2, axis=1, keepdims=True)
        o = _dot(p1.astype(BF16), vp[:, sl]) + _dot(p2.astype(BF16), vn[:, sl])
        o_ref[0, :, sl] = (o / l).astype(BF16)


def _attn_sample(q, kn, lat_buf, layer, latp, kpep, wuk, wuv):
    B, H, Ls, _ = q.shape
    P = latp.shape[1]
    full = _resident
    return pl.pallas_call(
        _attn_sample_kernel,
        out_shape=jax.ShapeDtypeStruct((B, Ls, H * MLA_V), BF16),
        grid=(B,),
        in_specs=[
            pl.BlockSpec((1, H, Ls, QK_PAD), lambda b: (b, 0, 0, 0)),
            pl.BlockSpec((1, H, Ls, QK_PAD), lambda b: (b, 0, 0, 0)),
            pl.BlockSpec((1, 1, Ls, MLA_KV_RANK), lambda b: (layer, b, 0, 0)),
            pl.BlockSpec((1, P, MLA_KV_RANK), lambda b: (b, 0, 0)),
            pl.BlockSpec((1, P, MLA_ROPE), lambda b: (b, 0, 0)),
            full(wuk.shape), full(wuv.shape),
        ],
        out_specs=pl.BlockSpec((1, Ls, H * MLA_V), lambda b: (b, 0, 0)),
        compiler_params=_params("parallel"), name="attn_sample",
    )(q, kn, lat_buf, latp, kpep, wuk, wuv)


DIAG_BLOCK = SUBLANES
HGRN_UNROLL = 8


def _hgrn_kernel(h4_ref, gn_ref, s0_ref, sbuf_in, o_ref, sout_ref, st_ref, pb_ref, ob_ref, lvl_ref,
                 *, c, nchunk, unroll):
    del sbuf_in
    t = pl.program_id(1)

    @pl.when(t == 0)
    def _():
        for h in range(HG_HEADS):
            st_ref[h] = s0_ref[0, h].T
        rt4 = lax.broadcasted_iota(jnp.int32, (HG_HEADS * c, HG_HEADS * c), 0)
        ct4 = lax.broadcasted_iota(jnp.int32, (HG_HEADS * c, HG_HEADS * c), 1)
        lvl = jnp.zeros((HG_HEADS * c, HG_HEADS * c), jnp.int32)
        n, level = c // 2, 0
        while n >= DIAG_BLOCK:
            level += 1
            hit = ((rt4 // (2 * n)) == (ct4 // (2 * n))) & ((rt4 % (2 * n)) >= n) & ((ct4 % (2 * n)) < n)
            lvl = jnp.where(hit, level, lvl)
            n //= 2
        lvl_ref[...] = lvl

    gn = gn_ref[...]

    rt = lax.broadcasted_iota(jnp.int32, (c, c), 0)
    ct = lax.broadcasted_iota(jnp.int32, (c, c), 1)
    tril = jnp.where(ct <= rt, 1.0, 0.0).astype(BF16)
    tril3 = jnp.concatenate([tril, tril, tril], axis=1)
    nblk = c // DIAG_BLOCK

    def chunk(ci, u):
        r0 = pl.multiple_of(ci * c, c)

        def seg(k):
            return h4_ref[0, pl.ds(r0, c), pl.ds(k * HG_W, HG_W)]

        def stack(w):
            return jnp.concatenate([w[:, h * HG_DK:(h + 1) * HG_DK] for h in range(HG_HEADS)], axis=0)

        qq_w, f_w, hi_w, gs_w = seg(0), seg(1), seg(2), seg(3)
        logf = jnp.log(f_w)
        kk_w = 1.0 - f_w
        vb_w = hi_w.astype(BF16)

        p0 = logf.astype(BF16)
        r1 = logf - p0.astype(F32)
        p1 = r1.astype(BF16)
        p2 = (r1 - p1.astype(F32)).astype(BF16)
        b_w = _dot(tril3, jnp.concatenate([p0, p1, p2], axis=0))
        bend_w = b_w[c - 1:c, :]
        ebend_w = jnp.exp(bend_w)

        inter = _dot_nt(stack((qq_w * jnp.exp(b_w)).astype(BF16)),
                        st_ref[...].reshape(HG_HEADS * HG_DV, HG_DK).astype(BF16))
        kd_w = (kk_w * jnp.exp(bend_w - b_w)).astype(BF16)
        upd = lax.dot_general(vb_w, kd_w, TN_DIMS, preferred_element_type=F32)
        for h in range(HG_HEADS):
            hs = slice(h * HG_DK, (h + 1) * HG_DK)
            st_ref[h] = st_ref[h] * ebend_w[:, hs] + upd[hs, hs]

        a_all = jnp.zeros((HG_HEADS * c, HG_HEADS * c), F32)
        n = c // 2
        level = 0
        while n >= DIAG_BLOCK:
            qparts, kparts = [], []
            for j in range(c // (2 * n)):
                lo, mid, hi_ = 2 * n * j, 2 * n * j + n, 2 * n * (j + 1)
                ref = b_w[mid - 1:mid, :]
                qparts += [jnp.zeros((n, HG_W), F32), qq_w[mid:hi_] * jnp.exp(b_w[mid:hi_] - ref)]
                kparts += [kk_w[lo:mid] * jnp.exp(ref - b_w[lo:mid]), jnp.zeros((n, HG_W), F32)]
            an = _dot_nt(stack(jnp.concatenate(qparts, axis=0)).astype(BF16),
                         stack(jnp.concatenate(kparts, axis=0)).astype(BF16))
            level += 1
            a_all = jnp.where(lvl_ref[...] == level, an, a_all)
            n //= 2
        intra = _dot(a_all.astype(BF16), stack(vb_w))

        for h in range(HG_HEADS):
            sl = u * HG_HEADS + h
            hs = slice(h * HG_DK, (h + 1) * HG_DK)
            qq, kk, b, hi = qq_w[:, hs], kk_w[:, hs], b_w[:, hs], hi_w[:, hs]
            o = inter[h * c:(h + 1) * c, hs] + intra[h * c:(h + 1) * c]

            pb_ref[sl, 0] = qq
            pb_ref[sl, 1] = kk
            pb_ref[sl, 2] = b
            pb_ref[sl, 3] = hi
            rows = [[pb_ref[sl, k, pl.ds(a, nblk, stride=DIAG_BLOCK), :] for a in range(DIAG_BLOCK)] for k in range(4)]
            for a in range(DIAG_BLOCK):
                oa = jnp.zeros((nblk, HG_DV), F32)
                for e_ in range(a + 1):
                    w = rows[0][a] * rows[1][e_]
                    if e_ < a:
                        w = w * jnp.exp(rows[2][a] - rows[2][e_])
                    oa = oa + jnp.sum(w, axis=1, keepdims=True) * rows[3][e_]
                ob_ref[sl, pl.ds(a, nblk, stride=DIAG_BLOCK), :] = oa
            o = o + ob_ref[sl]

            ob = _rms(o, gn) * gs_w[:, hs]
            o_ref[0, pl.ds(r0, c), pl.ds(h * HG_DV, HG_DV)] = ob.astype(BF16)

    def unrolled(cu, carry):
        for u in range(unroll):
            chunk(cu * unroll + u, u)
        return carry

    lax.fori_loop(0, nchunk // unroll, unrolled, 0)

    @pl.when(t == pl.num_programs(1) - 1)
    def _():
        for h in range(HG_HEADS):
            sout_ref[0, 0, h] = st_ref[h].T


def _hgrn(h4, gnorm, s0, s_buf, layer, tt):
    B, L, _ = h4.shape
    c = min(CHUNK, L)
    tt = min(tt, L)
    nchunk = tt // c
    unroll = HGRN_UNROLL if nchunk % HGRN_UNROLL == 0 else 1
    full = _resident
    return pl.pallas_call(
        functools.partial(_hgrn_kernel, c=c, nchunk=nchunk, unroll=unroll),
        out_shape=(jax.ShapeDtypeStruct((B, L, HG_HEADS * HG_DV), BF16),
                   jax.ShapeDtypeStruct(s_buf.shape, F32)),
        grid=(B, L // tt),
        in_specs=[
            pl.BlockSpec((1, tt, 4 * HG_W), lambda b, t: (b, t, 0)),
            full(gnorm.shape),
            pl.BlockSpec((1, HG_HEADS, HG_DK, HG_DV), lambda b, t: (b, 0, 0, 0)),
            _aliased(),
        ],
        out_specs=(pl.BlockSpec((1, tt, HG_HEADS * HG_DV), lambda b, t: (b, t, 0)),
                   pl.BlockSpec((1, 1, HG_HEADS, HG_DK, HG_DV), lambda b, t: (layer, b, 0, 0, 0))),
        scratch_shapes=[pltpu.VMEM((HG_HEADS, HG_DV, HG_DK), F32),
                        pltpu.VMEM((unroll * HG_HEADS, 4, c, HG_DK), F32),
                        pltpu.VMEM((unroll * HG_HEADS, c, HG_DV), F32),
                        pltpu.VMEM((HG_HEADS * c, HG_HEADS * c), jnp.int32)],
        input_output_aliases={3: 1},
        compiler_params=_params("parallel", "arbitrary"), name="hgrn",
    )(h4, gnorm, s0, s_buf)


def _merge_kernel(x_ref, oa_ref, ob_ref, g_ref, wpa_ref, wpb_ref, wout_ref, y_ref, *, sb, tm):
    rows = sb * tm
    g = g_ref[...].reshape(rows, 2 * D_MODEL).astype(F32)
    oa = oa_ref[...].reshape(rows, MLA_HEADS * MLA_V)
    ob = ob_ref[...].reshape(rows, HG_HEADS * HG_DV)
    mix = g[:, :D_MODEL] * _dot(oa, wpa_ref[...]) + g[:, D_MODEL:] * _dot(ob, wpb_ref[...])
    y = x_ref[...].reshape(rows, D_MODEL) + _dot(mix.astype(BF16), wout_ref[...])
    y_ref[...] = y.reshape(sb, tm, D_MODEL)


def _merge(x, oa, ob, g, wpa, wpb, wout, sb, tm):
    B, L, _ = x.shape
    full = _resident
    row = lambda w: pl.BlockSpec((sb, tm, w), lambda b, t: (b, t, 0))
    return pl.pallas_call(
        functools.partial(_merge_kernel, sb=sb, tm=tm),
        out_shape=jax.ShapeDtypeStruct(x.shape, F32),
        grid=(B // sb, L // tm),
        in_specs=[row(D_MODEL), row(oa.shape[2]), row(ob.shape[2]), row(2 * D_MODEL),
                  full(wpa.shape), full(wpb.shape), full(wout.shape)],
        out_specs=row(D_MODEL),
        compiler_params=_params("parallel", "parallel"), name="merge",
    )(x, oa, ob, g, wpa, wpb, wout)


FFN_SPLIT = 1
FFN_COLS = FFN_DIM // FFN_SPLIT


def _gelu_tanh(x):
    return 0.5 * x * (1.0 + jnp.tanh(math.sqrt(2.0 / math.pi) * (x + 0.044715 * (x * x * x))))


def _ffn_kernel(x_ref, cs_ref, nf_ref, wup_ref, cw_ref, cb_ref, wdn_ref, nfin_ref, cbuf_in, y_ref, cnew_ref,
                carry_ref, *, final, sb, tm):
    del cbuf_in
    t = pl.program_id(1)
    rows = sb * tm

    @pl.when(t == 0)
    def _():
        carry_ref[:, 0:CONV_W - 1, :] = cs_ref[...]

    x = x_ref[...].reshape(rows, D_MODEL)
    hb = _rms(x, nf_ref[...]).astype(BF16)
    pos = lax.broadcasted_iota(jnp.int32, (sb, tm, FFN_COLS), 1).reshape(rows, FFN_COLS)

    def per_row(v):
        return jnp.broadcast_to(v, (sb, tm, FFN_COLS)).reshape(rows, FFN_COLS)

    acc = x
    for c in range(FFN_SPLIT):
        cs = slice(c * FFN_COLS, (c + 1) * FFN_COLS)
        a = _dot(hb, wup_ref[:, c * FFN_COLS:(c + 1) * FFN_COLS])
        val = _dot(hb, wup_ref[:, FFN_DIM + c * FFN_COLS:FFN_DIM + (c + 1) * FFN_COLS])
        prev2 = per_row(carry_ref[:, 0:1, cs])
        prev1 = per_row(carry_ref[:, 1:2, cs])
        a1 = jnp.where(pos == 0, prev1, pltpu.roll(a, 1, 0))
        a2 = jnp.where(pos == 0, prev2, jnp.where(pos == 1, prev1, pltpu.roll(a, 2, 0)))
        conv = cb_ref[:, cs] + cw_ref[0:1, cs] * a2 + cw_ref[1:2, cs] * a1 + cw_ref[2:3, cs] * a
        gated = (_gelu_tanh(conv) * val).astype(BF16)
        acc = acc + _dot(gated, wdn_ref[c * FFN_COLS:(c + 1) * FFN_COLS, :])
        carry_ref[:, 0:CONV_W - 1, cs] = a.reshape(sb, tm, FFN_COLS)[:, tm - (CONV_W - 1):tm, :]
    if final:
        acc = _rms(acc, nfin_ref[...])
    y_ref[...] = acc.reshape(sb, tm, D_MODEL)

    @pl.when(t == pl.num_programs(1) - 1)
    def _():
        cnew_ref[0] = carry_ref[:, 0:CONV_W - 1, :]


def _ffn(x, conv_state, nf, wup, cw, cb, wdn, nfin, c_buf, layer, final, sb, tm):
    B, L, _ = x.shape
    full = _resident
    return pl.pallas_call(
        functools.partial(_ffn_kernel, final=final, sb=sb, tm=tm),
        out_shape=(jax.ShapeDtypeStruct(x.shape, F32), jax.ShapeDtypeStruct(c_buf.shape, F32)),
        grid=(B // sb, L // tm),
        in_specs=[
            pl.BlockSpec((sb, tm, D_MODEL), lambda b, t: (b, t, 0)),
            pl.BlockSpec((sb, CONV_W - 1, FFN_DIM), lambda b, t: (b, 0, 0)),
            full(nf.shape), full(wup.shape), full(cw.shape), full(cb.shape), full(wdn.shape), full(nfin.shape),
            _aliased(),
        ],
        out_specs=(pl.BlockSpec((sb, tm, D_MODEL), lambda b, t: (b, t, 0)),
                   pl.BlockSpec((1, sb, CONV_W - 1, FFN_DIM), lambda b, t: (layer, b, 0, 0))),
        scratch_shapes=[pltpu.VMEM((sb, SUBLANES, FFN_DIM), F32)],
        input_output_aliases={8: 1},
        compiler_params=_params("parallel", "arbitrary"), name="ffn",
    )(x, conv_state, nf, wup, cw, cb, wdn, nfin, c_buf)


def _rope_tables(pos):
    half = MLA_ROPE // 2
    inv = ROPE_BASE ** (-jnp.arange(half, dtype=F32) / half)
    ang = pos.astype(F32)[:, None] * inv[None, :]
    cos, sin = jnp.cos(ang), jnp.sin(ang)
    return jnp.tile(cos, (1, 4)), jnp.concatenate([-sin, sin, -sin, sin], axis=1)


def _prep_layer(w_in_l, w_uq_l, w_ukv_l):
    kr = w_in_l[:, OFF_KR:OFF_KR + MLA_ROPE]
    win = jnp.concatenate([w_in_l[:, :OFF_KR], kr, kr, w_in_l[:, OFF_KR + MLA_ROPE:]], axis=1).astype(BF16)
    uq = w_uq_l.reshape(MLA_Q_RANK, MLA_HEADS, MLA_QK)
    wuq = jnp.concatenate([uq[:, :, :MLA_NOPE].reshape(MLA_Q_RANK, -1),
                           uq[:, :, MLA_NOPE:].reshape(MLA_Q_RANK, -1)], axis=1).astype(BF16)
    ukv = w_ukv_l.reshape(MLA_KV_RANK, MLA_HEADS, MLA_NOPE + MLA_V)
    wuk = ukv[:, :, :MLA_NOPE].reshape(MLA_KV_RANK, -1).astype(BF16)
    wuv = ukv[:, :, MLA_NOPE:].reshape(MLA_KV_RANK, -1).astype(BF16)
    return win, wuq, wuk, wuv


def _tiling(B, L):
    if L % ROW_TILE == 0:
        return 1, ROW_TILE
    sb = 1
    if L % SUBLANES == 0:
        for cand in range(1, B + 1):
            if B % cand == 0 and cand * L <= ROW_TILE:
                sb = cand
    return sb, L


def _state_buffers(depth, B, L):
    return (jnp.zeros((depth, B, L, MLA_KV_RANK), F32), jnp.zeros((depth, B, L, MLA_ROPE), F32),
            jnp.zeros((depth, B, HG_HEADS, HG_DK, HG_DV), F32), jnp.zeros((depth, B, CONV_W - 1, FFN_DIM), F32))


def kernel(x_prompt, x_sample, cache_mla_latent, cache_mla_krope, state_hgrn, state_ffn_conv, norm_mix, w_in, q_norm, w_uq, kv_norm, w_ukv, lb_logits, hgrn_norm, w_proj_a, w_proj_b, w_out, norm_ffn, w_up, conv_w, conv_b, w_down, norm_final):
    depth = w_in.shape[0]
    B, Lp, _ = x_prompt.shape
    Bs, Ls, _ = x_sample.shape
    past = cache_mla_latent.shape[2]

    sb_p, tm_p = _tiling(B, Lp)
    sb_s, tm_s = _tiling(Bs, Ls)
    cos_p, sin_p = _rope_tables(jnp.arange(Lp))
    cos_s, sin_s = _rope_tables(past + jnp.arange(Ls))
    cos_s, sin_s = jnp.tile(cos_s, (sb_s, 1)), jnp.tile(sin_s, (sb_s, 1))
    s0_p = jnp.zeros((B, HG_HEADS, HG_DK, HG_DV), F32)
    conv0_p = jnp.zeros((B, CONV_W - 1, FFN_DIM), F32)
    nfin = norm_final.reshape(1, D_MODEL)
    lbl = lb_logits.astype(F32)

    tk = ATTN_TK if Lp % ATTN_TK == 0 else Lp

    lat_p, kpe_p, hs_p, cv_p = _state_buffers(depth, B, Lp)
    lat_s, kpe_s, hs_s, cv_s = _state_buffers(depth, Bs, Ls)

    xp, xs = x_prompt, x_sample
    for l in range(depth):
        win, wuq, wuk, wuv = _prep_layer(w_in[l], w_uq[l], w_ukv[l])
        wuvt = wuv.T
        nmix = norm_mix[l].reshape(1, -1)
        qn = q_norm[l].reshape(1, -1)
        kvn = kv_norm[l].reshape(1, -1)
        gn = hgrn_norm[l].reshape(1, -1)
        nf = norm_ffn[l].reshape(1, -1)
        wpa, wpb, wo = w_proj_a[l].astype(BF16), w_proj_b[l].astype(BF16), w_out[l].astype(BF16)
        wup, wdn = w_up[l].astype(BF16), w_down[l].astype(BF16)
        cw, cb = conv_w[l], conv_b[l].reshape(1, -1)
        final = l == depth - 1

        q, k, vt, lat_p, kpe_p, h4, g = _inproj(xp, cos_p, sin_p, nmix, win, qn, wuq, kvn, wuk, wuvt, lbl,
                                                lat_p, kpe_p, l, sb_p, tm_p, True)
        oa = _attn_prompt(q, k, vt, tk)
        ob, hs_p = _hgrn(h4, gn, s0_p, hs_p, l, 512)
        x1 = _merge(xp, oa, ob, g, wpa, wpb, wo, sb_p, tm_p)
        xp, cv_p = _ffn(x1, conv0_p, nf, wup, cw, cb, wdn, nfin, cv_p, l, final, sb_p, tm_p)

        q, k, lat_s, kpe_s, h4, g = _inproj(xs, cos_s, sin_s, nmix, win, qn, wuq, kvn, wuk, wuvt, lbl,
                                            lat_s, kpe_s, l, sb_s, tm_s, False)
        oa = _attn_sample(q, k, lat_s, l, cache_mla_latent[l], cache_mla_krope[l], wuk, wuv)
        ob, hs_s = _hgrn(h4, gn, state_hgrn[l], hs_s, l, 512)
        x1 = _merge(xs, oa, ob, g, wpa, wpb, wo, sb_s, tm_s)
        xs, cv_s = _ffn(x1, state_ffn_conv[l], nf, wup, cw, cb, wdn, nfin, cv_s, l, final, sb_s, tm_s)

    return (xp, xs, lat_p, kpe_p, hs_p, cv_p, lat_s, kpe_s, hs_s, cv_s)
```
